```python
import math
import jax, jax.numpy as jnp
from jax import lax
import numpy as np

D_MODEL = 1024
BATCH = 2
SEQ = 8192
DEPTH = 4
DEC_BATCH = 128
DEC_SEQ = 4
PAST_LEN = 2048
PAGE_SIZE = 128

N_HEADS = 16
HEAD_DIM = D_MODEL // N_HEADS
DIFF_QK_DIM = HEAD_DIM // 2
N_MIXERS = 3
N_DIFF_LAYERS = DEPTH // N_MIXERS
MOBA_BLOCK = 256
MOBA_TOPK = 3
MOBA_Q_CHUNK = 64
Q_BLOCK = 128
N_BUCKETS = 32
MAX_DISTANCE = 128
RMS_EPS = 1e-6
SUBLN_EPS = 1e-5

kernel_name = "moba_stickbreak_diffattn_hybrid_step"


def rmsnorm(x, g, eps):
    xf = x.astype(jnp.float32)
    r = lax.rsqrt(jnp.mean(xf * xf, axis=-1, keepdims=True) + eps)
    return (xf * r).astype(x.dtype) * g


def t5_bucket(n):
    n = jnp.maximum(n, 0)
    max_exact = N_BUCKETS // 2
    nf = jnp.maximum(n, 1).astype(jnp.float32)
    large = max_exact + (jnp.log(nf / max_exact) / math.log(MAX_DISTANCE / max_exact)
                         * (N_BUCKETS - max_exact)).astype(jnp.int32)
    large = jnp.minimum(large, N_BUCKETS - 1)
    return jnp.where(n < max_exact, n, large)


def over_query_blocks(fn, q, qpos):
    B, Q = q.shape[:2]
    if Q <= Q_BLOCK or Q % Q_BLOCK:
        return fn(q, qpos)
    nb = Q // Q_BLOCK
    qb = jnp.moveaxis(q.reshape(B, nb, Q_BLOCK, *q.shape[2:]), 1, 0)
    pb = qpos.reshape(nb, Q_BLOCK)
    ob = lax.map(lambda a: fn(a[0], a[1]), (qb, pb))
    return jnp.moveaxis(ob, 0, 1).reshape(B, Q, *ob.shape[3:])


def stick_breaking(q, k, v, qpos, kpos):
    z = jnp.einsum('bqhd,bkhd->bhqk', q, k).astype(jnp.float32) * (HEAD_DIM ** -0.5)
    past = kpos[None, :] < qpos[:, None]
    log_not = jnp.where(past, jax.nn.log_sigmoid(-z), 0.0)
    suffix = lax.cumsum(log_not, axis=3, reverse=True) - log_not
    a = jnp.where(past, jnp.exp(jax.nn.log_sigmoid(z) + suffix), 0.0)
    return jnp.einsum('bhqk,bkhd->bqhd', a.astype(v.dtype), v)


def diff_attention(q, k, v, qpos, kpos, bias_table, lam, lambda_init, subln_g):
    B, Q = q.shape[:2]
    L = k.shape[1]
    q2 = q.reshape(B, Q, N_HEADS, 2, DIFF_QK_DIM)
    k2 = k.reshape(B, L, N_HEADS, 2, DIFF_QK_DIM)
    bias = jnp.transpose(bias_table[t5_bucket(qpos[:, None] - kpos[None, :])], (2, 0, 1))
    s = jnp.einsum('bqhcd,bkhcd->bchqk', q2, k2).astype(jnp.float32) * (DIFF_QK_DIM ** -0.5) + bias
    causal = kpos[None, :] <= qpos[:, None]
    p = jax.nn.softmax(jnp.where(causal, s, -jnp.inf), axis=-1)
    attn = p[:, 0] - lam * p[:, 1]
    o = jnp.einsum('bhqk,bkhd->bqhd', attn.astype(v.dtype), v)
    return rmsnorm(o, subln_g, SUBLN_EPS) * (1.0 - lambda_init)


def to_blocks(x):
    B, L = x.shape[:2]
    nb = -(-L // MOBA_BLOCK)
    x = jnp.pad(x, ((0, 0), (0, nb * MOBA_BLOCK - L), (0, 0), (0, 0)))
    return x.reshape(B, nb, MOBA_BLOCK, *x.shape[2:])


def moba_chunk(q, qpos, kbt, vbt, kmean, bias_table):
    n_blocks = kbt.shape[1]
    topk = min(MOBA_TOPK, n_blocks)
    qblk = qpos // MOBA_BLOCK
    gate = jnp.einsum('chd,hnd->hcn', q, kmean).astype(jnp.float32)
    fully_past = jnp.arange(n_blocks)[None, None, :] < qblk[None, :, None]
    gate = jnp.where(fully_past, gate, -jnp.inf)
    _, top = lax.top_k(gate, topk)
    own = jnp.broadcast_to(qblk[None, :, None], (N_HEADS, q.shape[0], 1)).astype(top.dtype)
    sel = jnp.concatenate([top, own], axis=-1)
    h_idx = jnp.arange(N_HEADS)[:, None, None]
    k_sel = kbt[h_idx, sel]
    v_sel = vbt[h_idx, sel]
    kpos = sel[..., None] * MOBA_BLOCK + jnp.arange(MOBA_BLOCK)
    is_own = (jnp.arange(topk + 1) == topk)[None, None, :, None]
    valid = jnp.where(is_own, kpos <= qpos[None, :, None, None],
                      (sel < qblk[None, :, None])[..., None])
    bias = bias_table.T[h_idx[..., None], t5_bucket(qpos[None, :, None, None] - kpos)]
    logits = jnp.einsum('chd,hcjbd->hcjb', q, k_sel).astype(jnp.float32) * (HEAD_DIM ** -0.5) + bias
    logits = jnp.where(valid, logits, -jnp.inf)
    H_, C_, J_, BL = logits.shape
    p = jax.nn.softmax(logits.reshape(H_, C_, J_ * BL), axis=-1).reshape(H_, C_, J_, BL)
    return jnp.einsum('hcjb,hcjbd->chd', p.astype(v_sel.dtype), v_sel)


def moba_attention(q, k, v, qpos, bias_table):
    B, Q = q.shape[:2]
    kb, vb = to_blocks(k), to_blocks(v)
    kmean = jnp.mean(kb.astype(jnp.float32), axis=2).astype(k.dtype)
    chunk = MOBA_Q_CHUNK if Q % MOBA_Q_CHUNK == 0 else Q
    n_chunks = Q // chunk
    qc = q.reshape(B, n_chunks, chunk, N_HEADS, HEAD_DIM)
    pc = qpos.reshape(n_chunks, chunk)

    def per_seq(args):
        q_s, kb_s, vb_s, km_s = args
        kbt = jnp.transpose(kb_s, (2, 0, 1, 3))
        vbt = jnp.transpose(vb_s, (2, 0, 1, 3))
        kmt = jnp.transpose(km_s, (1, 0, 2))
        return lax.map(lambda a: moba_chunk(a[0], a[1], kbt, vbt, kmt, bias_table), (q_s, pc))

    o = lax.map(per_seq, (qc, kb, vb, kmean))
    return o.reshape(B, Q, N_HEADS, HEAD_DIM)


def trunk(x, qpos, cache_k, cache_v, page_table, bias_table, norm_g, w_in, w_out,
          lq1, lk1, lq2, lk2, subln_g, final_g):
    B, Q, _ = x.shape
    new_k, new_v = [], []
    for i in range(DEPTH):
        h = rmsnorm(x, norm_g[i], RMS_EPS)
        q, k, v, gate = jnp.split(h @ w_in[i], 4, axis=-1)
        q, k, v = [t.reshape(B, Q, N_HEADS, HEAD_DIM) for t in (q, k, v)]
        new_k.append(k)
        new_v.append(v)
        if page_table is None:
            k_all, v_all = k, v
        else:
            k_past = cache_k[i, page_table].reshape(B, -1, N_HEADS, HEAD_DIM)
            v_past = cache_v[i, page_table].reshape(B, -1, N_HEADS, HEAD_DIM)
            k_all = jnp.concatenate([k_past, k], axis=1)
            v_all = jnp.concatenate([v_past, v], axis=1)
        kpos = jnp.arange(k_all.shape[1])
        kind = i % N_MIXERS
        if kind == 0:
            o = moba_attention(q, k_all, v_all, qpos, bias_table)
        elif kind == 1:
            o = over_query_blocks(lambda qb, pb: stick_breaking(qb, k_all, v_all, pb, kpos), q, qpos)
        else:
            j = i // N_MIXERS
            lambda_init = 0.8 - 0.6 * math.exp(-0.3 * i)
            lam = (jnp.exp(jnp.sum(lq1[j] * lk1[j]).astype(jnp.float32))
                   - jnp.exp(jnp.sum(lq2[j] * lk2[j]).astype(jnp.float32)) + lambda_init)
            o = over_query_blocks(
                lambda qb, pb: diff_attention(qb, k_all, v_all, pb, kpos, bias_table, lam,
                                              lambda_init, subln_g[j]), q, qpos)
        x = x + (o.reshape(B, Q, D_MODEL) * jax.nn.silu(gate)) @ w_out[i]
    return rmsnorm(x, final_g, RMS_EPS), jnp.stack(new_k), jnp.stack(new_v)


def setup_inputs(seed: int = 0) -> dict:
    key = jax.random.key(seed)
    ks = jax.random.split(key, 16)
    n_pages = PAST_LEN // PAGE_SIZE
    n_pool = (5 * DEC_BATCH * n_pages + 3) // 4
    f32 = jnp.float32
    x_prompt = jax.random.normal(ks[0], (BATCH, SEQ, D_MODEL), f32)
    x_sample = jax.random.normal(ks[1], (DEC_BATCH, DEC_SEQ, D_MODEL), f32)
    cache_k = jax.random.normal(ks[2], (DEPTH, n_pool, PAGE_SIZE, N_HEADS, HEAD_DIM), f32)
    cache_v = jax.random.normal(ks[3], (DEPTH, n_pool, PAGE_SIZE, N_HEADS, HEAD_DIM), f32)
    page_table = jax.random.permutation(ks[4], n_pool)[:DEC_BATCH * n_pages].reshape(
        DEC_BATCH, n_pages).astype(jnp.int32)
    bias_table = 0.2 * jax.random.normal(ks[5], (N_BUCKETS, N_HEADS), f32)
    norm_g = 1.0 + 0.02 * jax.random.normal(ks[6], (DEPTH, D_MODEL), f32)
    w_in = jax.random.normal(ks[7], (DEPTH, D_MODEL, 4 * D_MODEL), f32) * D_MODEL ** -0.5
    w_out = jax.random.normal(ks[8], (DEPTH, D_MODEL, D_MODEL), f32) * D_MODEL ** -0.5
    diff_lambda_q1 = 0.1 * jax.random.normal(ks[9], (N_DIFF_LAYERS, DIFF_QK_DIM), f32)
    diff_lambda_k1 = 0.1 * jax.random.normal(ks[10], (N_DIFF_LAYERS, DIFF_QK_DIM), f32)
    diff_lambda_q2 = 0.1 * jax.random.normal(ks[11], (N_DIFF_LAYERS, DIFF_QK_DIM), f32)
    diff_lambda_k2 = 0.1 * jax.random.normal(ks[12], (N_DIFF_LAYERS, DIFF_QK_DIM), f32)
    diff_subln_g = 1.0 + 0.02 * jax.random.normal(ks[13], (N_DIFF_LAYERS, HEAD_DIM), f32)
    final_norm_g = 1.0 + 0.02 * jax.random.normal(ks[14], (D_MODEL,), f32)
    return {"x_prompt": x_prompt, "x_sample": x_sample, "cache_k": cache_k, "cache_v": cache_v,
            "page_table": page_table, "bias_table": bias_table, "norm_g": norm_g,
            "w_in": w_in, "w_out": w_out, "diff_lambda_q1": diff_lambda_q1,
            "diff_lambda_k1": diff_lambda_k1, "diff_lambda_q2": diff_lambda_q2,
            "diff_lambda_k2": diff_lambda_k2, "diff_subln_g": diff_subln_g,
            "final_norm_g": final_norm_g}


def reference(x_prompt, x_sample, cache_k, cache_v, page_table, bias_table, norm_g, w_in, w_out,
              diff_lambda_q1, diff_lambda_k1, diff_lambda_q2, diff_lambda_k2, diff_subln_g,
              final_norm_g):
    qpos_prompt = jnp.arange(x_prompt.shape[1])
    y_prompt, new_k_prompt, new_v_prompt = trunk(
        x_prompt, qpos_prompt, None, None, None, bias_table, norm_g, w_in, w_out,
        diff_lambda_q1, diff_lambda_k1, diff_lambda_q2, diff_lambda_k2, diff_subln_g, final_norm_g)
    past_len = page_table.shape[1] * cache_k.shape[2]
    qpos_sample = past_len + jnp.arange(x_sample.shape[1])
    y_sample, new_k_sample, new_v_sample = trunk(
        x_sample, qpos_sample, cache_k, cache_v, page_table, bias_table, norm_g, w_in, w_out,
        diff_lambda_q1, diff_lambda_k1, diff_lambda_q2, diff_lambda_k2, diff_subln_g, final_norm_g)
    return (y_prompt, y_sample, new_k_prompt, new_v_prompt, new_k_sample, new_v_sample)
```

```python
import functools
import math

import jax
import jax.numpy as jnp
from jax import lax
from jax.experimental import pallas as pl
from jax.experimental.pallas import tpu as pltpu

N_HEADS = 16
HEAD_DIM = 64
D_MODEL = N_HEADS * HEAD_DIM
DIFF_QK_DIM = HEAD_DIM // 2
N_MIXERS = 3
MOBA_BLOCK = 256
MOBA_TOPK = 3
N_BUCKETS = 32
MAX_DISTANCE = 128
RMS_EPS = 1e-6
SUBLN_EPS = 1e-5

LANES = 128
HEADS_PER_TILE = LANES // HEAD_DIM
N_PAIRS = N_HEADS // HEADS_PER_TILE
TQ = MOBA_BLOCK
NEG = -1e30
VMEM_LIMIT = 48 * 1024 * 1024
F32 = jnp.float32
BF16 = jnp.bfloat16

_NT = (((1,), (1,)), ((), ()))


def _dot(a, b):
    return jnp.dot(a, b, preferred_element_type=F32)


def _dot_nt(a, b, precision=None):
    return lax.dot_general(a, b, _NT, preferred_element_type=F32, precision=precision)


def _silu(x):
    return x / (1.0 + jnp.exp(-x))


def _proj_kernel(*refs, has_out, has_in, tm):
    it = iter(refs)
    x_ref = next(it)
    og_ref = next(it) if has_out else None
    wo_ref = next(it) if has_out else None
    g_ref = next(it)
    wi_ref = next(it) if has_in else None
    if has_out and has_in:
        xn_ref = next(it)
    if has_in:
        q_ref, k_ref, v_ref, gt_ref, kb_ref, vb_ref, km_ref = (next(it) for _ in range(7))
    else:
        y_ref = next(it)

    x = x_ref[...]
    if has_out:
        x = x + _dot(og_ref[...].astype(BF16), wo_ref[...])
        if has_in:
            xn_ref[...] = x
    r = lax.rsqrt(jnp.mean(x * x, axis=-1, keepdims=True) + RMS_EPS)
    h = (x * r) * g_ref[...]
    if not has_in:
        y_ref[...] = h
        return
    acc = _dot(h.astype(BF16), wi_ref[...])
    q_ref[...] = acc[:, 0 * D_MODEL:1 * D_MODEL]
    k = acc[:, 1 * D_MODEL:2 * D_MODEL]
    v = acc[:, 2 * D_MODEL:3 * D_MODEL]
    k_ref[...] = k
    v_ref[...] = v
    gt_ref[...] = acc[:, 3 * D_MODEL:4 * D_MODEL]
    kb_ref[...] = k.astype(BF16)
    vb_ref[...] = v.astype(BF16)
    nblk = tm // MOBA_BLOCK
    if nblk:
        km_ref[...] = jnp.sum(k.reshape(nblk, MOBA_BLOCK, D_MODEL), axis=1) * (1.0 / MOBA_BLOCK)
    else:
        km_ref[...] = jnp.zeros(km_ref.shape, F32)


def _proj(x, og, w_out, g, w_in):
    T = x.shape[0]
    tm = min(512, T)
    assert T % tm == 0
    has_out, has_in = og is not None, w_in is not None
    row = lambda i: (i, 0)
    fixed = lambda i: (0, 0)
    args, in_specs = [x], [pl.BlockSpec((tm, D_MODEL), row)]
    if has_out:
        args += [og, w_out]
        in_specs += [pl.BlockSpec((tm, D_MODEL), row), pl.BlockSpec((D_MODEL, D_MODEL), fixed)]
    args.append(g.reshape(1, D_MODEL))
    in_specs.append(pl.BlockSpec((1, D_MODEL), fixed))
    out_shape, out_specs = [], []
    if has_in:
        args.append(w_in)
        in_specs.append(pl.BlockSpec((D_MODEL, 4 * D_MODEL), fixed))
        if has_out:
            out_shape.append(jax.ShapeDtypeStruct((T, D_MODEL), F32))
            out_specs.append(pl.BlockSpec((tm, D_MODEL), row))
        for dt in (F32, F32, F32, F32, BF16, BF16):
            out_shape.append(jax.ShapeDtypeStruct((T, D_MODEL), dt))
            out_specs.append(pl.BlockSpec((tm, D_MODEL), row))
        nblk = max(tm // MOBA_BLOCK, 1)
        out_shape.append(jax.ShapeDtypeStruct((T // tm, nblk, D_MODEL), F32))
        out_specs.append(pl.BlockSpec((None, nblk, D_MODEL), lambda i: (i, 0, 0)))
    else:
        out_shape.append(jax.ShapeDtypeStruct((T, D_MODEL), F32))
        out_specs.append(pl.BlockSpec((tm, D_MODEL), row))
    outs = pl.pallas_call(
        functools.partial(_proj_kernel, has_out=has_out, has_in=has_in, tm=tm),
        grid=(T // tm,),
        in_specs=in_specs,
        out_specs=out_specs,
        out_shape=out_shape,
        compiler_params=pltpu.CompilerParams(
            dimension_semantics=("arbitrary",), vmem_limit_bytes=VMEM_LIMIT),
        name=f"proj_o{int(has_out)}_i{int(has_in)}",
    )(*args)
    return outs


def _t5_bucket(n):
    n = jnp.maximum(n, 0)
    max_exact = N_BUCKETS // 2
    nf = jnp.maximum(n, 1).astype(F32)
    large = max_exact + (jnp.log(nf / max_exact) / math.log(MAX_DISTANCE / max_exact)
                         * (N_BUCKETS - max_exact)).astype(jnp.int32)
    large = jnp.minimum(large, N_BUCKETS - 1)
    return jnp.where(n < max_exact, n, large)


def _rel_bias(bias_table, dist):
    b = bias_table[_t5_bucket(dist)] - bias_table[N_BUCKETS - 1]
    return jnp.moveaxis(b, -1, 0)


def _prompt_bias_tiles(bias_table):
    qi = jnp.arange(TQ)[:, None]
    ki = jnp.arange(TQ)[None, :]
    diag = jnp.where(ki <= qi, _rel_bias(bias_table, qi - ki), NEG)
    adj = _rel_bias(bias_table, TQ + qi - ki)
    return diag.astype(F32), adj.astype(F32)


def _head_lane_masks(shape):
    lane = lax.broadcasted_iota(jnp.int32, shape, len(shape) - 1)
    return [(lane >= h * HEAD_DIM) & (lane < (h + 1) * HEAD_DIM) for h in range(HEADS_PER_TILE)]


def _softmax_step(s, v, m, l, acc):
    m_new = jnp.maximum(m, jnp.max(s, axis=1, keepdims=True))
    alpha = jnp.exp(m - m_new)
    p = jnp.exp(s - m_new)
    l = alpha * l + jnp.sum(p, axis=1, keepdims=True)
    acc = alpha * acc + _dot(p.astype(BF16), v)
    return m_new, l, acc


def _moba_kernel(q_ref, k_ref, v_ref, km_ref, gt_ref, td_ref, ta_ref, o_ref):
    i = pl.program_id(2)
    q = q_ref[...]
    hmask = _head_lane_masks(q.shape)
    n_iota = lax.broadcasted_iota(jnp.int32, (LANES, TQ), 0)
    valid = n_iota < i
    lane = lax.broadcasted_iota(jnp.int32, (TQ, LANES), 1)
    zeros_b = jnp.zeros((TQ, LANES), BF16)

    q_aug = []
    for h in range(HEADS_PER_TILE):
        qm = jnp.where(hmask[h], q, 0.0)
        g = _dot_nt(km_ref[...], qm, precision=lax.Precision.HIGHEST)
        g = jnp.where(valid, g, -jnp.inf)
        sel = jnp.zeros(g.shape, jnp.bool_)
        for _ in range(MOBA_TOPK):
            mx = jnp.max(g, axis=0, keepdims=True)
            cand = (g == mx) & valid & jnp.logical_not(sel)
            idx = jnp.min(jnp.where(cand, n_iota, LANES), axis=0, keepdims=True)
            pick = n_iota == idx
            sel = sel | pick
            g = jnp.where(pick, -jnp.inf, g)
        selb = jnp.where(sel, 0.0, NEG).T
        q_aug.append(jnp.concatenate(
            [(qm * HEAD_DIM ** -0.5).astype(BF16), selb.astype(BF16)], axis=1))

    def step(j, carry, bias, onehot):
        kj = k_ref[pl.ds(pl.multiple_of(j * TQ, TQ), TQ), :]
        vj = v_ref[pl.ds(pl.multiple_of(j * TQ, TQ), TQ), :]
        ej = jnp.where(lane == j, 1.0, 0.0).astype(BF16) if onehot else zeros_b
        k_aug = jnp.concatenate([kj, ej], axis=1)
        out = []
        for h in range(HEADS_PER_TILE):
            s = _dot_nt(q_aug[h], k_aug)
            if bias is not None:
                s = s + bias[h]
            out.append(_softmax_step(s, vj, *carry[h]))
        return tuple(out)

    init = tuple((jnp.full((TQ, 1), NEG, F32), jnp.zeros((TQ, 1), F32),
                  jnp.zeros((TQ, LANES), F32)) for _ in range(HEADS_PER_TILE))
    carry = step(i, init, td_ref, False)
    carry = step(jnp.maximum(i - 1, 0), carry, ta_ref, True)
    carry = lax.fori_loop(0, jnp.maximum(i - 1, 0),
                          lambda j, c: step(j, c, None, True), carry)
    o = jnp.zeros((TQ, LANES), F32)
    for h in range(HEADS_PER_TILE):
        m, l, acc = carry[h]
        o = jnp.where(hmask[h], acc / l, o)
    o_ref[...] = (o * _silu(gt_ref[...])).astype(BF16)


def _stick_kernel(q_ref, k_ref, v_ref, gt_ref, u_ref, o_ref):
    i = pl.program_id(2)
    q = q_ref[...]
    hmask = _head_lane_masks(q.shape)
    qs = [(jnp.where(hmask[h], q, 0.0) * HEAD_DIM ** -0.5).astype(BF16)
          for h in range(HEADS_PER_TILE)]
    qi = lax.broadcasted_iota(jnp.int32, (TQ, TQ), 0)
    ki = lax.broadcasted_iota(jnp.int32, (TQ, TQ), 1)
    past = ki < qi

    def step(j, carry, diag):
        kj = k_ref[pl.ds(pl.multiple_of(j * TQ, TQ), TQ), :]
        vj = v_ref[pl.ds(pl.multiple_of(j * TQ, TQ), TQ), :]
        out = []
        for h in range(HEADS_PER_TILE):
            c, acc = carry[h]
            z = _dot_nt(qs[h], kj)
            sp = jnp.maximum(z, 0.0) + jnp.log(1.0 + jnp.exp(-jnp.abs(z)))
            ln = -sp
            if diag:
                ln = jnp.where(past, ln, 0.0)
            hi = ln.astype(BF16)
            lo = (ln - hi.astype(F32)).astype(BF16)
            suf = _dot(jnp.concatenate([hi, lo], axis=1), u_ref[...])
            a = jnp.exp((z - sp) + suf + c)
            if diag:
                a = jnp.where(past, a, 0.0)
            acc = acc + _dot(a.astype(BF16), vj)
            c = c + jnp.sum(ln, axis=1, keepdims=True)
            out.append((c, acc))
        return tuple(out)

    init = tuple((jnp.zeros((TQ, 1), F32), jnp.zeros((TQ, LANES), F32))
                 for _ in range(HEADS_PER_TILE))
    carry = step(i, init, True)
    carry = lax.fori_loop(0, i, lambda t, c: step(i - 1 - t, c, False), carry)
    o = jnp.zeros((TQ, LANES), F32)
    for h in range(HEADS_PER_TILE):
        o = jnp.where(hmask[h], carry[h][1], o)
    o_ref[...] = (o * _silu(gt_ref[...])).astype(BF16)


def _diff_finish(o, hmask, sg_ref, lambda_init):
    o2 = o * o
    r = jnp.zeros(o.shape, F32)
    for h in range(HEADS_PER_TILE):
        ms = jnp.sum(jnp.where(hmask[h], o2, 0.0), axis=1, keepdims=True) * (1.0 / HEAD_DIM)
        r = jnp.where(hmask[h], lax.rsqrt(ms + SUBLN_EPS), r)
    return (o * r) * sg_ref[...] * (1.0 - lambda_init)


def _diff_kernel(lam_ref, q_ref, k_ref, v_ref, gt_ref, td_ref, ta_ref, sg_ref, o_ref, *,
                 lambda_init):
    i = pl.program_id(2)
    q = q_ref[...]
    hmask = _head_lane_masks(q.shape)
    lane = lax.broadcasted_iota(jnp.int32, q.shape, 1)
    qm = []
    for h in range(HEADS_PER_TILE):
        for c in range(2):
            lo = h * HEAD_DIM + c * DIFF_QK_DIM
            msk = (lane >= lo) & (lane < lo + DIFF_QK_DIM)
            qm.append((jnp.where(msk, q, 0.0) * DIFF_QK_DIM ** -0.5).astype(BF16))

    def step(j, carry, bias):
        kj = k_ref[pl.ds(pl.multiple_of(j * TQ, TQ), TQ), :]
        vj = v_ref[pl.ds(pl.multiple_of(j * TQ, TQ), TQ), :]
        out = []
        for n in range(2 * HEADS_PER_TILE):
            s = _dot_nt(qm[n], kj)
            if bias is not None:
                s = s + bias[n // 2]
            out.append(_softmax_step(s, vj, *carry[n]))
        return tuple(out)

    init = tuple((jnp.full((TQ, 1), NEG, F32), jnp.zeros((TQ, 1), F32),
                  jnp.zeros((TQ, LANES), F32)) for _ in range(2 * HEADS_PER_TILE))
    carry = step(i, init, td_ref)
    adj = step(jnp.maximum(i - 1, 0), carry, ta_ref)
    carry = jax.tree_util.tree_map(lambda n, o: jnp.where(i > 0, n, o), adj, carry)
    carry = lax.fori_loop(0, jnp.maximum(i - 1, 0), lambda j, c: step(j, c, None), carry)
    lam = lam_ref[0]
    o = jnp.zeros((TQ, LANES), F32)
    for h in range(HEADS_PER_TILE):
        (_, l0, a0), (_, l1, a1) = carry[2 * h], carry[2 * h + 1]
        o = jnp.where(hmask[h], a0 / l0 - lam * (a1 / l1), o)
    o = _diff_finish(o, hmask, sg_ref, lambda_init)
    o_ref[...] = (o * _silu(gt_ref[...])).astype(BF16)


def _prompt_attention(kind, q, kb, vb, gate, kmean, tiles, extra, B, S):
    nq = S // TQ
    q3, g3 = q.reshape(B, S, D_MODEL), gate.reshape(B, S, D_MODEL)
    k3, v3 = kb.reshape(B, S, D_MODEL), vb.reshape(B, S, D_MODEL)
    tile_spec = pl.BlockSpec((None, TQ, LANES), lambda b, p, i, *_: (b, i, p))
    seq_spec = pl.BlockSpec((None, S, LANES), lambda b, p, i, *_: (b, 0, p))
    bias_spec = pl.BlockSpec((HEADS_PER_TILE, TQ, TQ), lambda b, p, i, *_: (p, 0, 0))
    td, ta = tiles
    num_prefetch = 0
    if kind == 0:
        nb_pad = LANES
        km = jnp.pad(kmean.reshape(B, nq, D_MODEL), ((0, 0), (0, nb_pad - nq), (0, 0)))
        body = _moba_kernel
        args = [q3, k3, v3, km, g3, td, ta]
        in_specs = [tile_spec, seq_spec, seq_spec,
                    pl.BlockSpec((None, nb_pad, LANES), lambda b, p, i: (b, 0, p)),
                    tile_spec, bias_spec, bias_spec]
    elif kind == 1:
        r = jnp.arange(2 * TQ)[:, None] % TQ
        u = (r > jnp.arange(TQ)[None, :]).astype(BF16)
        body = _stick_kernel
        args = [q3, k3, v3, g3, u]
        in_specs = [tile_spec, seq_spec, seq_spec, tile_spec,
                    pl.BlockSpec((2 * TQ, TQ), lambda b, p, i: (0, 0))]
    else:
        lam, lambda_init, subln_g = extra
        sg = jnp.tile(subln_g, HEADS_PER_TILE).reshape(1, LANES)
        body = functools.partial(_diff_kernel, lambda_init=lambda_init)
        args = [lam.reshape(1), q3, k3, v3, g3, td, ta, sg]
        num_prefetch = 1
        in_specs = [tile_spec, seq_spec, seq_spec, tile_spec, bias_spec, bias_spec,
                    pl.BlockSpec((1, LANES), lambda b, p, i, *_: (0, 0))]
    og = pl.pallas_call(
        body,
        grid_spec=pltpu.PrefetchScalarGridSpec(
            num_scalar_prefetch=num_prefetch,
            grid=(B, N_PAIRS, nq),
            in_specs=in_specs,
            out_specs=tile_spec),
        out_shape=jax.ShapeDtypeStruct((B, S, D_MODEL), BF16),
        compiler_params=pltpu.CompilerParams(
            dimension_semantics=("arbitrary", "arbitrary", "arbitrary"),
            vmem_limit_bytes=VMEM_LIMIT),
        name=("moba", "stick", "diff")[kind] + "_prompt",
    )(*args)
    return og.reshape(B * S, D_MODEL)


def _diff_lambda(i, lq1, lk1, lq2, lk2):
    j = i // N_MIXERS
    lambda_init = 0.8 - 0.6 * math.exp(-0.3 * i)
    lam = (jnp.exp(jnp.sum(lq1[j] * lk1[j]).astype(F32))
           - jnp.exp(jnp.sum(lq2[j] * lk2[j]).astype(F32)) + lambda_init)
    return lam.astype(F32), lambda_init


def _prompt_trunk(x_prompt, bias_table, norm_g, w_in, w_out, lq1, lk1, lq2, lk2, subln_g, final_g):
    B, S, _ = x_prompt.shape
    depth = w_in.shape[0]
    assert S % TQ == 0 and S // TQ <= LANES
    tiles = _prompt_bias_tiles(bias_table)
    x = x_prompt.reshape(B * S, D_MODEL)
    og = None
    new_k, new_v = [], []
    for i in range(depth):
        outs = _proj(x, og, None if og is None else w_out[i - 1].astype(BF16),
                     norm_g[i], w_in[i].astype(BF16))
        if og is not None:
            x, outs = outs[0], outs[1:]
        q, k, v, gate, kb, vb, kmean = outs
        new_k.append(k.reshape(B, S, N_HEADS, HEAD_DIM))
        new_v.append(v.reshape(B, S, N_HEADS, HEAD_DIM))
        kind = i % N_MIXERS
        extra = None
        if kind == 2:
            lam, lambda_init = _diff_lambda(i, lq1, lk1, lq2, lk2)
            extra = (lam, lambda_init, subln_g[i // N_MIXERS])
        og = _prompt_attention(kind, q, kb, vb, gate, kmean, tiles, extra, B, S)
    y = _proj(x, og, w_out[depth - 1].astype(BF16), final_g, None)[0]
    return y.reshape(B, S, D_MODEL), jnp.stack(new_k), jnp.stack(new_v)


NEW_PAD = 8


def _sample_rows(q, n_maps):
    ds = q.shape[0]
    lane = lax.broadcasted_iota(jnp.int32, (N_HEADS, D_MODEL), 1)
    head = lax.broadcasted_iota(jnp.int32, (N_HEADS, D_MODEL), 0)
    width = HEAD_DIM // n_maps
    rows = []
    for c in range(n_maps):
        lo = head * HEAD_DIM + c * width
        msk = (lane >= lo) & (lane < lo + width)
        for qi in range(ds):
            rows.append(jnp.where(msk, jnp.broadcast_to(q[qi:qi + 1, :], msk.shape), 0.0))
    return jnp.concatenate(rows, axis=0)


def _own_head_mask(ds):
    r = lax.broadcasted_iota(jnp.int32, (ds * N_HEADS, D_MODEL), 0)
    lane = lax.broadcasted_iota(jnp.int32, (ds * N_HEADS, D_MODEL), 1)
    return (r % N_HEADS) == (lane // HEAD_DIM)


def _extract_heads(o, own):
    ds = o.shape[0] // N_HEADS
    return jnp.sum(jnp.where(own, o, 0.0).reshape(ds, N_HEADS, D_MODEL), axis=1)


def _new_tokens(kn_ref, vn_ref, width):
    pad = jnp.zeros((width - NEW_PAD, D_MODEL), F32)
    kn = jnp.concatenate([kn_ref[...], pad], axis=0).astype(BF16)
    vn = jnp.concatenate([vn_ref[...], pad], axis=0).astype(BF16)
    return kn, vn


def _block_kv(ka_ref, kb_ref, va_ref, vb_ref):
    k2 = jnp.concatenate([ka_ref[...], kb_ref[...]], axis=0)
    v2 = jnp.concatenate([va_ref[...], vb_ref[...]], axis=0)
    return k2, v2.astype(BF16)


def _moba_sample_kernel(pt_ref, q_ref, kn_ref, vn_ref, gt_ref, ka_ref, kb_ref, va_ref, vb_ref,
                        bt_ref, bo_ref, o_ref, qf_sc, qb_sc, m_sc, l_sc, g_sc, acc_sc, *, nblk, ds):
    t = pl.program_id(1)
    rows = ds * N_HEADS
    lane = lax.broadcasted_iota(jnp.int32, (rows, LANES), 1)

    @pl.when(t == 0)
    def _():
        r = _sample_rows(q_ref[...], 1)
        qf_sc[...] = r
        qb_sc[...] = (r * HEAD_DIM ** -0.5).astype(BF16)
        m_sc[...] = jnp.full(m_sc.shape, NEG, F32)
        l_sc[...] = jnp.zeros(l_sc.shape, F32)
        g_sc[...] = jnp.full(g_sc.shape, -jnp.inf, F32)

    k2, v2 = _block_kv(ka_ref, kb_ref, va_ref, vb_ref)
    s = _dot_nt(qb_sc[...], k2.astype(BF16)) + bt_ref[...]
    m = jnp.max(s, axis=1, keepdims=True)
    p = jnp.exp(s - m)
    l = jnp.sum(p, axis=1, keepdims=True)
    acc_sc[t] = _dot(p.astype(BF16), v2)
    ksum = jnp.sum(k2, axis=0, keepdims=True)
    gate = jnp.sum(qf_sc[...] * ksum, axis=1, keepdims=True) * (1.0 / MOBA_BLOCK)
    m_sc[...] = jnp.where(lane == t, m, m_sc[...])
    l_sc[...] = jnp.where(lane == t, l, l_sc[...])
    g_sc[...] = jnp.where(lane == t, gate, g_sc[...])

    @pl.when(t == nblk - 1)
    def _():
        valid = lane < nblk
        g = g_sc[...]
        sel = jnp.zeros(g.shape, jnp.bool_)
        for _ in range(MOBA_TOPK):
            mx = jnp.max(g, axis=1, keepdims=True)
            cand = (g == mx) & valid & jnp.logical_not(sel)
            idx = jnp.min(jnp.where(cand, lane, LANES), axis=1, keepdims=True)
            pick = lane == idx
            sel = sel | pick
            g = jnp.where(pick, -jnp.inf, g)
        kn, vn = _new_tokens(kn_ref, vn_ref, LANES)
        so = _dot_nt(qb_sc[...], kn) + bo_ref[...]
        mo = jnp.max(so, axis=1, keepdims=True)
        po = jnp.exp(so - mo)
        lo = jnp.sum(po, axis=1, keepdims=True)
        acc_o = _dot(po.astype(BF16), vn)
        mb = m_sc[...]
        mt = jnp.maximum(mo, jnp.max(jnp.where(sel, mb, NEG), axis=1, keepdims=True))
        w = jnp.where(sel, jnp.exp(mb - mt), 0.0)
        wo = jnp.exp(mo - mt)
        lt = wo * lo + jnp.sum(w * l_sc[...], axis=1, keepdims=True)
        o = wo * acc_o
        for n in range(nblk):
            o = o + w[:, n:n + 1] * acc_sc[n]
        o = _extract_heads(o / lt, _own_head_mask(ds))
        o_ref[...] = o * _silu(gt_ref[...])


def _stick_sample_kernel(pt_ref, q_ref, kn_ref, vn_ref, gt_ref, ka_ref, kb_ref, va_ref, vb_ref,
                         uo_ref, ub_ref, o_ref, qb_sc, c_sc, acc_sc, *, nblk, ds):
    t = pl.program_id(1)
    rows = ds * N_HEADS

    def weights(z, c, u_ref, past):
        sp = jnp.maximum(z, 0.0) + jnp.log(1.0 + jnp.exp(-jnp.abs(z)))
        ln = -sp
        if past is not None:
            ln = jnp.where(past, ln, 0.0)
        hi = ln.astype(BF16)
        lo = (ln - hi.astype(F32)).astype(BF16)
        suf = _dot(jnp.concatenate([hi, lo], axis=1), u_ref[...])
        a = jnp.exp((z - sp) + suf + c)
        if past is not None:
            a = jnp.where(past, a, 0.0)
        return a.astype(BF16), jnp.sum(ln, axis=1, keepdims=True)

    @pl.when(t == 0)
    def _():
        qb = (_sample_rows(q_ref[...], 1) * HEAD_DIM ** -0.5).astype(BF16)
        qb_sc[...] = qb
        kn, vn = _new_tokens(kn_ref, vn_ref, LANES)
        qi = lax.broadcasted_iota(jnp.int32, (rows, LANES), 0) // N_HEADS
        ki = lax.broadcasted_iota(jnp.int32, (rows, LANES), 1)
        a, dc = weights(_dot_nt(qb, kn), 0.0, uo_ref, ki < qi)
        acc_sc[...] = _dot(a, vn)
        c_sc[...] = dc

    k2, v2 = _block_kv(ka_ref, kb_ref, va_ref, vb_ref)
    a, dc = weights(_dot_nt(qb_sc[...], k2.astype(BF16)), c_sc[...], ub_ref, None)
    acc_sc[...] += _dot(a, v2)
    c_sc[...] += dc

    @pl.when(t == nblk - 1)
    def _():
        o_ref[...] = _extract_heads(acc_sc[...], _own_head_mask(ds)) * _silu(gt_ref[...])


def _diff_sample_kernel(pt_ref, lam_ref, q_ref, kn_ref, vn_ref, gt_ref, ka_ref, kb_ref, va_ref, vb_ref,
                        bt_ref, bo_ref, sg_ref, o_ref, qb_sc, m_sc, l_sc, acc_sc, *,
                        nblk, ds, lambda_init):
    t = pl.program_id(1)
    rows = ds * N_HEADS

    @pl.when(t == 0)
    def _():
        qb = (_sample_rows(q_ref[...], 2) * DIFF_QK_DIM ** -0.5).astype(BF16)
        qb_sc[...] = qb
        kn, vn = _new_tokens(kn_ref, vn_ref, LANES)
        so = _dot_nt(qb, kn) + bo_ref[...]
        mo = jnp.max(so, axis=1, keepdims=True)
        po = jnp.exp(so - mo)
        m_sc[...] = mo
        l_sc[...] = jnp.sum(po, axis=1, keepdims=True)
        acc_sc[...] = _dot(po.astype(BF16), vn)

    k2, v2 = _block_kv(ka_ref, kb_ref, va_ref, vb_ref)
    s = _dot_nt(qb_sc[...], k2.astype(BF16)) + bt_ref[...]
    m, l, acc = _softmax_step(s, v2, m_sc[...], l_sc[...], acc_sc[...])
    m_sc[...] = m
    l_sc[...] = l
    acc_sc[...] = acc

    @pl.when(t == nblk - 1)
    def _():
        o = acc_sc[...] / l_sc[...]
        o = o[:rows] - lam_ref[0] * o[rows:]
        own = _own_head_mask(ds)
        ms = jnp.sum(jnp.where(own, o * o, 0.0), axis=1, keepdims=True) * (1.0 / HEAD_DIM)
        o = _extract_heads(o * lax.rsqrt(ms + SUBLN_EPS), own)
        o = o * sg_ref[...] * (1.0 - lambda_init)
        o_ref[...] = o * _silu(gt_ref[...])


def _sample_bias_tables(bias_table, past, ds, nblk, n_maps):
    qi = jnp.arange(ds)
    kk = jnp.arange(MOBA_BLOCK)
    dist = past + qi[None, :, None] - (jnp.arange(nblk)[:, None, None] * MOBA_BLOCK + kk[None, None, :])
    bt = jnp.transpose(_rel_bias(bias_table, dist), (1, 2, 0, 3))
    bt = bt.reshape(nblk, ds * N_HEADS, MOBA_BLOCK)
    ki = jnp.arange(LANES)
    ok = (ki[None, :] <= qi[:, None]) & (ki[None, :] < ds)
    bo = jnp.where(ok[None], _rel_bias(bias_table, qi[:, None] - ki[None, :]), NEG)
    bo = jnp.transpose(bo, (1, 0, 2)).reshape(ds * N_HEADS, LANES)
    return (jnp.tile(bt, (1, n_maps, 1)).astype(F32), jnp.tile(bo, (n_maps, 1)).astype(F32))


def _sample_attention(kind, layer, q, k, v, gate, cache_k, cache_v, page_table, bias_table, extra):
    DB, n_pages = page_table.shape
    page = cache_k.shape[2]
    ds = q.shape[0] // DB
    ppb = MOBA_BLOCK // page
    assert ppb == 2 and n_pages % ppb == 0 and ds <= NEW_PAD
    nblk = n_pages // ppb
    past = n_pages * page
    rows = ds * N_HEADS
    pt = page_table.reshape(-1).astype(jnp.int32)
    q3, g3 = q.reshape(DB, ds, D_MODEL), gate.reshape(DB, ds, D_MODEL)
    padn = ((0, 0), (0, NEW_PAD - ds), (0, 0))
    kn3 = jnp.pad(k.reshape(DB, ds, D_MODEL), padn)
    vn3 = jnp.pad(v.reshape(DB, ds, D_MODEL), padn)

    reverse = kind == 1

    def page_spec(which):
        def index(b, t, pt_ref, *_):
            blk = (nblk - 1 - t) if reverse else t
            return (layer, pt_ref[b * n_pages + blk * ppb + which], 0, 0)
        return pl.BlockSpec((None, None, page, D_MODEL), index)

    tok_spec = pl.BlockSpec((None, ds, D_MODEL), lambda b, t, *_: (b, 0, 0))
    new_spec = pl.BlockSpec((None, NEW_PAD, D_MODEL), lambda b, t, *_: (b, 0, 0))
    fixed2 = lambda shape: pl.BlockSpec(shape, lambda b, t, *_: (0, 0))
    pages = [cache_k, cache_k, cache_v, cache_v]
    page_specs = [page_spec(0), page_spec(1), page_spec(0), page_spec(1)]
    vm = lambda shape, dt=F32: pltpu.VMEM(shape, dt)
    if kind == 0:
        bt, bo = _sample_bias_tables(bias_table, past, ds, nblk, 1)
        body = functools.partial(_moba_sample_kernel, nblk=nblk, ds=ds)
        args = [pt, q3, kn3, vn3, g3, *pages, bt, bo]
        in_specs = [tok_spec, new_spec, new_spec, tok_spec, *page_specs,
                    pl.BlockSpec((None, rows, MOBA_BLOCK), lambda b, t, *_: (t, 0, 0)),
                    fixed2((rows, LANES))]
        scratch = [vm((rows, D_MODEL)), vm((rows, D_MODEL), BF16), vm((rows, LANES)), vm((rows, LANES)),
                   vm((rows, LANES)), vm((nblk, rows, D_MODEL))]
        num_prefetch = 1
    elif kind == 1:
        def umat(n):
            r = jnp.arange(2 * n)[:, None] % n
            return (r > jnp.arange(n)[None, :]).astype(BF16)
        body = functools.partial(_stick_sample_kernel, nblk=nblk, ds=ds)
        args = [pt, q3, kn3, vn3, g3, *pages, umat(LANES), umat(MOBA_BLOCK)]
        in_specs = [tok_spec, new_spec, new_spec, tok_spec, *page_specs,
                    fixed2((2 * LANES, LANES)), fixed2((2 * MOBA_BLOCK, MOBA_BLOCK))]
        scratch = [vm((rows, D_MODEL), BF16), vm((rows, 1)), vm((rows, D_MODEL))]
        num_prefetch = 1
    else:
        lam, lambda_init, subln_g = extra
        bt, bo = _sample_bias_tables(bias_table, past, ds, nblk, 2)
        sg = jnp.tile(subln_g, N_HEADS).reshape(1, D_MODEL)
        body = functools.partial(_diff_sample_kernel, nblk=nblk, ds=ds, lambda_init=lambda_init)
        args = [pt, lam.reshape(1), q3, kn3, vn3, g3, *pages, bt, bo, sg]
        in_specs = [tok_spec, new_spec, new_spec, tok_spec, *page_specs,
                    pl.BlockSpec((None, 2 * rows, MOBA_BLOCK), lambda b, t, *_: (t, 0, 0)),
                    fixed2((2 * rows, LANES)), fixed2((1, D_MODEL))]
        scratch = [vm((2 * rows, D_MODEL), BF16), vm((2 * rows, 1)), vm((2 * rows, 1)),
                   vm((2 * rows, D_MODEL))]
        num_prefetch = 2
    og = pl.pallas_call(
        body,
        grid_spec=pltpu.PrefetchScalarGridSpec(
            num_scalar_prefetch=num_prefetch,
            grid=(DB, nblk),
            in_specs=in_specs,
            out_specs=tok_spec,
            scratch_shapes=scratch),
        out_shape=jax.ShapeDtypeStruct((DB, ds, D_MODEL), F32),
        compiler_params=pltpu.CompilerParams(
            dimension_semantics=("arbitrary", "arbitrary"), vmem_limit_bytes=VMEM_LIMIT),
        name=("moba", "stick", "diff")[kind] + "_sample",
    )(*args)
    return og.reshape(DB * ds, D_MODEL)


def _sample_trunk(x_sample, cache_k, cache_v, page_table, bias_table, norm_g, w_in, w_out,
                  lq1, lk1, lq2, lk2, subln_g, final_g):
    DB, ds, _ = x_sample.shape
    depth = w_in.shape[0]
    ck = cache_k.reshape(*cache_k.shape[:3], D_MODEL)
    cv = cache_v.reshape(*cache_v.shape[:3], D_MODEL)
    x = x_sample.reshape(DB * ds, D_MODEL)
    og = None
    new_k, new_v = [], []
    for i in range(depth):
        outs = _proj(x, og, None if og is None else w_out[i - 1].astype(BF16),
                     norm_g[i], w_in[i].astype(BF16))
        if og is not None:
            x, outs = outs[0], outs[1:]
        q, k, v, gate = outs[:4]
        new_k.append(k.reshape(DB, ds, N_HEADS, HEAD_DIM))
        new_v.append(v.reshape(DB, ds, N_HEADS, HEAD_DIM))
        kind = i % N_MIXERS
        extra = None
        if kind == 2:
            lam, lambda_init = _diff_lambda(i, lq1, lk1, lq2, lk2)
            extra = (lam, lambda_init, subln_g[i // N_MIXERS])
        og = _sample_attention(kind, i, q, k, v, gate, ck, cv, page_table, bias_table, extra)
    y = _proj(x, og, w_out[depth - 1].astype(BF16), final_g, None)[0]
    return y.reshape(DB, ds, D_MODEL), jnp.stack(new_k), jnp.stack(new_v)


def kernel(x_prompt, x_sample, cache_k, cache_v, page_table, bias_table, norm_g, w_in, w_out,
           diff_lambda_q1, diff_lambda_k1, diff_lambda_q2, diff_lambda_k2, diff_subln_g,
           final_norm_g):
    lams = (diff_lambda_q1, diff_lambda_k1, diff_lambda_q2, diff_lambda_k2)
    y_p, k_p, v_p = _prompt_trunk(x_prompt, bias_table, norm_g, w_in, w_out, *lams,
                                  diff_subln_g, final_norm_g)
    y_s, k_s, v_s = _sample_trunk(x_sample, cache_k, cache_v, page_table, bias_table, norm_g,
                                  w_in, w_out, *lams, diff_subln_g, final_norm_g)
    return (y_p, y_s, k_p, v_p, k_s, v_s)
```

```python
import functools
import math

import jax
import jax.numpy as jnp
from jax import lax
from jax.experimental import pallas as pl
from jax.experimental.pallas import tpu as pltpu

N_HEADS = 16
HEAD_DIM = 64
D_MODEL = N_HEADS * HEAD_DIM
DIFF_QK_DIM = HEAD_DIM // 2
N_MIXERS = 3
MOBA_BLOCK = 256
MOBA_TOPK = 3
N_BUCKETS = 32
MAX_DISTANCE = 128
RMS_EPS = 1e-6
SUBLN_EPS = 1e-5

LANES = 128
HEADS_PER_TILE = LANES // HEAD_DIM
N_PAIRS = N_HEADS // HEADS_PER_TILE
TQ = MOBA_BLOCK
NEG = -1e30
LOG2E = math.log2(math.e)
EXIT_LOG2 = -160.0
VMEM_LIMIT = 48 * 1024 * 1024
F32 = jnp.float32
BF16 = jnp.bfloat16

def _dot(a, b):
    return jnp.dot(a, b, preferred_element_type=F32)


def _silu(x):
    return x / (1.0 + jnp.exp(-x))


def _proj_kernel(*refs, has_out, has_in, tm):
    it = iter(refs)
    x_ref = next(it)
    og_ref = next(it) if has_out else None
    wo_ref = next(it) if has_out else None
    g_ref = next(it)
    wi_ref = next(it) if has_in else None
    if has_out and has_in:
        xn_ref = next(it)
    if has_in:
        q_ref, k_ref, v_ref, gt_ref, kb_ref, vt_ref, km_ref = (next(it) for _ in range(7))
    else:
        y_ref = next(it)

    x = x_ref[...]
    if has_out:
        x = x + _dot(og_ref[...].astype(BF16), wo_ref[...])
        if has_in:
            xn_ref[...] = x
    r = lax.rsqrt(jnp.mean(x * x, axis=-1, keepdims=True) + RMS_EPS)
    h = (x * r) * g_ref[...]
    if not has_in:
        y_ref[...] = h
        return
    acc = _dot(h.astype(BF16), wi_ref[...])
    q_ref[...] = acc[:, 0 * D_MODEL:1 * D_MODEL]
    k = acc[:, 1 * D_MODEL:2 * D_MODEL]
    v = acc[:, 2 * D_MODEL:3 * D_MODEL]
    k_ref[...] = k
    v_ref[...] = v
    gt_ref[...] = acc[:, 3 * D_MODEL:4 * D_MODEL]
    kb_ref[...] = k.astype(BF16)
    nblk = tm // MOBA_BLOCK
    if nblk:
        km_ref[...] = jnp.sum(k.reshape(nblk, MOBA_BLOCK, D_MODEL), axis=1) * (1.0 / MOBA_BLOCK)
        row = lax.broadcasted_iota(jnp.int32, (N_PAIRS, LANES, MOBA_BLOCK), 1)
        for jb in range(nblk):
            vt = v[jb * MOBA_BLOCK:(jb + 1) * MOBA_BLOCK, :].T.reshape(N_PAIRS, LANES, MOBA_BLOCK)
            vt_ref[jb] = jnp.concatenate(
                [jnp.where(row < HEAD_DIM, vt, 0.0), jnp.where(row >= HEAD_DIM, vt, 0.0)],
                axis=2).astype(BF16)
    else:
        km_ref[...] = jnp.zeros(km_ref.shape, F32)
        vt_ref[...] = jnp.zeros(vt_ref.shape, BF16)


def _proj(x, og, w_out, g, w_in):
    T = x.shape[0]
    tm = min(512, T)
    assert T % tm == 0
    has_out, has_in = og is not None, w_in is not None
    row = lambda i: (i, 0)
    fixed = lambda i: (0, 0)
    args, in_specs = [x], [pl.BlockSpec((tm, D_MODEL), row)]
    if has_out:
        args += [og, w_out]
        in_specs += [pl.BlockSpec((tm, D_MODEL), row), pl.BlockSpec((D_MODEL, D_MODEL), fixed)]
    args.append(g.reshape(1, D_MODEL))
    in_specs.append(pl.BlockSpec((1, D_MODEL), fixed))
    out_shape, out_specs = [], []
    if has_in:
        args.append(w_in)
        in_specs.append(pl.BlockSpec((D_MODEL, 4 * D_MODEL), fixed))
        if has_out:
            out_shape.append(jax.ShapeDtypeStruct((T, D_MODEL), F32))
            out_specs.append(pl.BlockSpec((tm, D_MODEL), row))
        for dt in (F32, F32, F32, F32, BF16):
            out_shape.append(jax.ShapeDtypeStruct((T, D_MODEL), dt))
            out_specs.append(pl.BlockSpec((tm, D_MODEL), row))
        nblk = max(tm // MOBA_BLOCK, 1)
        out_shape.append(jax.ShapeDtypeStruct(
            (T // tm * nblk, N_PAIRS, LANES, 2 * MOBA_BLOCK), BF16))
        out_specs.append(pl.BlockSpec((nblk, N_PAIRS, LANES, 2 * MOBA_BLOCK),
                                      lambda i: (i, 0, 0, 0)))
        out_shape.append(jax.ShapeDtypeStruct((T // tm, nblk, D_MODEL), F32))
        out_specs.append(pl.BlockSpec((None, nblk, D_MODEL), lambda i: (i, 0, 0)))
    else:
        out_shape.append(jax.ShapeDtypeStruct((T, D_MODEL), F32))
        out_specs.append(pl.BlockSpec((tm, D_MODEL), row))
    outs = pl.pallas_call(
        functools.partial(_proj_kernel, has_out=has_out, has_in=has_in, tm=tm),
        grid=(T // tm,),
        in_specs=in_specs,
        out_specs=out_specs,
        out_shape=out_shape,
        compiler_params=pltpu.CompilerParams(
            dimension_semantics=("arbitrary",), vmem_limit_bytes=VMEM_LIMIT),
        name=f"proj_o{int(has_out)}_i{int(has_in)}",
    )(*args)
    return outs


def _t5_bucket(n):
    n = jnp.maximum(n, 0)
    max_exact = N_BUCKETS // 2
    nf = jnp.maximum(n, 1).astype(F32)
    large = max_exact + (jnp.log(nf / max_exact) / math.log(MAX_DISTANCE / max_exact)
                         * (N_BUCKETS - max_exact)).astype(jnp.int32)
    large = jnp.minimum(large, N_BUCKETS - 1)
    return jnp.where(n < max_exact, n, large)


def _rel_bias(bias_table, dist):
    b = bias_table[_t5_bucket(dist)] - bias_table[N_BUCKETS - 1]
    return jnp.moveaxis(b, -1, 0)


def _prompt_bias_tiles(bias_table):
    ki = jnp.arange(TQ)[:, None]
    qi = jnp.arange(TQ)[None, :]
    diag = jnp.where(ki <= qi, _rel_bias(bias_table, qi - ki) * LOG2E, NEG)
    adj = _rel_bias(bias_table, TQ + qi - ki) * LOG2E
    return diag.astype(F32), adj.astype(F32)


def _head_lane_masks(shape):
    lane = lax.broadcasted_iota(jnp.int32, shape, len(shape) - 1)
    return [(lane >= h * HEAD_DIM) & (lane < (h + 1) * HEAD_DIM) for h in range(HEADS_PER_TILE)]


def _row_masks():
    row = lax.broadcasted_iota(jnp.int32, (LANES, TQ), 0)
    return [(row >= h * HEAD_DIM) & (row < (h + 1) * HEAD_DIM) for h in range(HEADS_PER_TILE)]


def _pair_rows(x0, x1):
    return jnp.concatenate([jnp.broadcast_to(x0, (HEAD_DIM, TQ)),
                            jnp.broadcast_to(x1, (HEAD_DIM, TQ))], axis=0)


def _softmax_step_t(s, m, l):
    m_new = jnp.maximum(m, jnp.max(s, axis=0, keepdims=True))
    alpha = jnp.exp2(m - m_new)
    p = jnp.exp2(s - m_new)
    l = alpha * l + jnp.sum(p, axis=0, keepdims=True)
    return m_new, l, alpha, p.astype(BF16)


def _key_block(k_ref, j):
    return k_ref[pl.ds(pl.multiple_of(j * TQ, TQ), TQ), :]


def _run_pipelined(scores, update, s0, carry, n):
    last = jnp.maximum(n - 1, 0)

    def body(j, state):
        s_cur, c = state
        s_next = scores(jnp.minimum(j + 1, last))
        return s_next, update(s_cur, j, c)

    return lax.fori_loop(0, n, body, (s0, carry))[1]


def _moba_kernel(q_ref, k_ref, vt_ref, km_ref, gt_ref, td_ref, ta_ref, o_ref, *, nbp):
    i = pl.program_id(2)
    qT = q_ref[...].T
    rmask = _row_masks()
    n_iota = lax.broadcasted_iota(jnp.int32, (nbp, TQ), 0)
    valid = n_iota < i
    lane = lax.broadcasted_iota(jnp.int32, (TQ, LANES), 1)
    zeros_b = jnp.zeros((TQ, LANES), BF16)

    q_aug = []
    for h in range(HEADS_PER_TILE):
        qm = jnp.where(rmask[h], qT, 0.0)
        g = jnp.dot(km_ref[...], qm, preferred_element_type=F32,
                    precision=lax.Precision.HIGHEST)
        g = jnp.where(valid, g, -jnp.inf)
        sel = jnp.zeros(g.shape, jnp.bool_)
        for _ in range(MOBA_TOPK):
            mx = jnp.max(g, axis=0, keepdims=True)
            cand = (g == mx) & valid & jnp.logical_not(sel)
            idx = jnp.min(jnp.where(cand, n_iota, nbp), axis=0, keepdims=True)
            pick = n_iota == idx
            sel = sel | pick
            g = jnp.where(pick, -jnp.inf, g)
        selb = jnp.where(sel, 0.0, NEG)
        q_aug.append(jnp.concatenate(
            [qm * (HEAD_DIM ** -0.5 * LOG2E), selb, jnp.zeros((LANES - nbp, TQ), F32)],
            axis=0).astype(BF16))

    def scores(j, onehot):
        ej = jnp.where(lane == j, 1.0, 0.0).astype(BF16) if onehot else zeros_b
        k_aug = jnp.concatenate([_key_block(k_ref, j), ej], axis=1)
        return tuple(_dot(k_aug, q_aug[h]) for h in range(HEADS_PER_TILE))

    def update(ss, j, carry, bias):
        stats, acc = carry
        new, alphas, ps = [], [], []
        for h in range(HEADS_PER_TILE):
            s = ss[h] if bias is None else ss[h] + bias[h]
            m, l, alpha, p = _softmax_step_t(s, *stats[h])
            new.append((m, l))
            alphas.append(alpha)
            ps.append(p)
        acc = acc * _pair_rows(*alphas) + _dot(vt_ref[j], jnp.concatenate(ps, axis=0))
        return tuple(new), acc

    init = (tuple((jnp.full((1, TQ), NEG, F32), jnp.zeros((1, TQ), F32))
                  for _ in range(HEADS_PER_TILE)), jnp.zeros((LANES, TQ), F32))
    prev = jnp.maximum(i - 1, 0)
    n_far = jnp.maximum(i - 1, 0)
    s_own = scores(i, False)
    s_prev = scores(prev, True)
    carry = update(s_own, i, init, td_ref)
    s_far = scores(0, True)
    carry = update(s_prev, prev, carry, ta_ref)
    stats, acc = _run_pipelined(lambda j: scores(j, True), lambda ss, j, c: update(ss, j, c, None),
                                s_far, carry, n_far)
    o = (acc * _pair_rows(1.0 / stats[0][1], 1.0 / stats[1][1])).T
    o_ref[...] = (o * _silu(gt_ref[...])).astype(BF16)


def _stick_kernel(q_ref, k_ref, vt_ref, gt_ref, u_ref, o_ref):
    i = pl.program_id(2)
    qT = q_ref[...].T
    rmask = _row_masks()
    qs = [(jnp.where(rmask[h], qT, 0.0) * (HEAD_DIM ** -0.5 * LOG2E)).astype(BF16)
          for h in range(HEADS_PER_TILE)]
    ki = lax.broadcasted_iota(jnp.int32, (TQ, TQ), 0)
    qi = lax.broadcasted_iota(jnp.int32, (TQ, TQ), 1)
    past = ki < qi

    def step(j, carry, diag):
        cs, acc = carry
        kj = _key_block(k_ref, j)
        new_c, ws = [], []
        for h in range(HEADS_PER_TILE):
            z = _dot(kj, qs[h])
            sp = jnp.maximum(z, 0.0) + jnp.log2(1.0 + jnp.exp2(-jnp.abs(z)))
            ln = -sp
            if diag:
                ln = jnp.where(past, ln, 0.0)
            hi = ln.astype(BF16)
            lo = (ln - hi.astype(F32)).astype(BF16)
            suf = _dot(u_ref[...], jnp.concatenate([hi, lo], axis=0))
            a = jnp.exp2((z - sp) + suf + cs[h])
            if diag:
                a = jnp.where(past, a, 0.0)
            ws.append(a.astype(BF16))
            new_c.append(cs[h] + jnp.sum(ln, axis=0, keepdims=True))
        acc = acc + _dot(vt_ref[j], jnp.concatenate(ws, axis=0))
        return tuple(new_c), acc

    def left(cs):
        return jnp.max(jnp.maximum(cs[0], cs[1]))

    init = (tuple(jnp.zeros((1, TQ), F32) for _ in range(HEADS_PER_TILE)),
            jnp.zeros((LANES, TQ), F32))
    carry = step(i, init, True)

    def cond(state):
        t, cmax, _ = state
        return (t < i) & (cmax > EXIT_LOG2)

    def body(state):
        t, _, c = state
        c = step(i - 1 - t, c, False)
        return t + 1, left(c[0]), c

    _, _, carry = lax.while_loop(cond, body, (jnp.int32(0), left(carry[0]), carry))
    o_ref[...] = (carry[1].T * _silu(gt_ref[...])).astype(BF16)


def _diff_finish(o, hmask, sg_ref, lambda_init):
    o2 = o * o
    r = jnp.zeros(o.shape, F32)
    for h in range(HEADS_PER_TILE):
        ms = jnp.sum(jnp.where(hmask[h], o2, 0.0), axis=1, keepdims=True) * (1.0 / HEAD_DIM)
        r = jnp.where(hmask[h], lax.rsqrt(ms + SUBLN_EPS), r)
    return (o * r) * sg_ref[...] * (1.0 - lambda_init)


def _diff_kernel(lam_ref, q_ref, k_ref, vt_ref, gt_ref, td_ref, ta_ref, sg_ref, o_ref, *,
                 lambda_init):
    i = pl.program_id(2)
    qT = q_ref[...].T
    row = lax.broadcasted_iota(jnp.int32, qT.shape, 0)
    qm = [[None, None] for _ in range(HEADS_PER_TILE)]
    for h in range(HEADS_PER_TILE):
        for c in range(2):
            lo = h * HEAD_DIM + c * DIFF_QK_DIM
            msk = (row >= lo) & (row < lo + DIFF_QK_DIM)
            qm[h][c] = (jnp.where(msk, qT, 0.0) * (DIFF_QK_DIM ** -0.5 * LOG2E)).astype(BF16)

    def scores(j):
        kj = _key_block(k_ref, j)
        return tuple(tuple(_dot(kj, qm[h][c]) for h in range(HEADS_PER_TILE))
                     for c in range(2))

    def update(ss, j, carry, bias):
        vt = vt_ref[j]
        out = []
        for c in range(2):
            stats, acc = carry[c]
            new, alphas, ps = [], [], []
            for h in range(HEADS_PER_TILE):
                s = ss[c][h] if bias is None else ss[c][h] + bias[h]
                m, l, alpha, p = _softmax_step_t(s, *stats[h])
                new.append((m, l))
                alphas.append(alpha)
                ps.append(p)
            acc = acc * _pair_rows(*alphas) + _dot(vt, jnp.concatenate(ps, axis=0))
            out.append((tuple(new), acc))
        return tuple(out)

    init = tuple((tuple((jnp.full((1, TQ), NEG, F32), jnp.zeros((1, TQ), F32))
                        for _ in range(HEADS_PER_TILE)), jnp.zeros((LANES, TQ), F32))
                 for _ in range(2))
    prev = jnp.maximum(i - 1, 0)
    s_own = scores(i)
    s_prev = scores(prev)
    carry = update(s_own, i, init, td_ref)
    s_far = scores(0)
    adj = update(s_prev, prev, carry, ta_ref)
    carry = jax.tree_util.tree_map(lambda n, o: jnp.where(i > 0, n, o), adj, carry)
    carry = _run_pipelined(scores, lambda ss, j, c: update(ss, j, c, None), s_far, carry,
                           jnp.maximum(i - 1, 0))
    outs =[acc * _pair_rows(1.0 / stats[0][1], 1.0 / stats[1][1]) for stats, acc in carry]
    o = (outs[0] - lam_ref[0] * outs[1]).T
    o = _diff_finish(o, _head_lane_masks(o.shape), sg_ref, lambda_init)
    o_ref[...] = (o * _silu(gt_ref[...])).astype(BF16)


def _prompt_attention(kind, q, kb, vt, gate, kmean, tiles, extra, B, S):
    nq = S // TQ
    q3, g3 = q.reshape(B, S, D_MODEL), gate.reshape(B, S, D_MODEL)
    k3 = kb.reshape(B, S, D_MODEL)
    v3 = vt.reshape(B, nq, N_PAIRS, LANES, 2 * TQ)
    tile_spec = pl.BlockSpec((None, TQ, LANES), lambda b, p, i, *_: (b, i, p))
    seq_spec = pl.BlockSpec((None, S, LANES), lambda b, p, i, *_: (b, 0, p))
    vt_spec = pl.BlockSpec((None, nq, None, LANES, 2 * TQ), lambda b, p, i, *_: (b, 0, p, 0, 0))
    bias_spec = pl.BlockSpec((HEADS_PER_TILE, TQ, TQ), lambda b, p, i, *_: (p, 0, 0))
    td, ta = tiles
    num_prefetch = 0
    if kind == 0:
        nbp = -(-nq // 8) * 8
        km = jnp.pad(kmean.reshape(B, nq, D_MODEL), ((0, 0), (0, nbp - nq), (0, 0)))
        body = functools.partial(_moba_kernel, nbp=nbp)
        args = [q3, k3, v3, km, g3, td, ta]
        in_specs = [tile_spec, seq_spec, vt_spec,
                    pl.BlockSpec((None, nbp, LANES), lambda b, p, i: (b, 0, p)),
                    tile_spec, bias_spec, bias_spec]
    elif kind == 1:
        c = jnp.arange(2 * TQ)[None, :] % TQ
        u = (c > jnp.arange(TQ)[:, None]).astype(BF16)
        body = _stick_kernel
        args = [q3, k3, v3, g3, u]
        in_specs = [tile_spec, seq_spec, vt_spec, tile_spec,
                    pl.BlockSpec((TQ, 2 * TQ), lambda b, p, i: (0, 0))]
    else:
        lam, lambda_init, subln_g = extra
        sg = jnp.tile(subln_g, HEADS_PER_TILE).reshape(1, LANES)
        body = functools.partial(_diff_kernel, lambda_init=lambda_init)
        args = [lam.reshape(1), q3, k3, v3, g3, td, ta, sg]
        num_prefetch = 1
        in_specs = [tile_spec, seq_spec, vt_spec, tile_spec, bias_spec, bias_spec,
                    pl.BlockSpec((1, LANES), lambda b, p, i, *_: (0, 0))]
    og = pl.pallas_call(
        body,
        grid_spec=pltpu.PrefetchScalarGridSpec(
            num_scalar_prefetch=num_prefetch,
            grid=(B, N_PAIRS, nq),
            in_specs=in_specs,
            out_specs=tile_spec),
        out_shape=jax.ShapeDtypeStruct((B, S, D_MODEL), BF16),
        compiler_params=pltpu.CompilerParams(
            dimension_semantics=("arbitrary", "arbitrary", "arbitrary"),
            vmem_limit_bytes=VMEM_LIMIT),
        name=("moba", "stick", "diff")[kind] + "_prompt",
    )(*args)
    return og.reshape(B * S, D_MODEL)


def _diff_lambda(i, lq1, lk1, lq2, lk2):
    j = i // N_MIXERS
    lambda_init = 0.8 - 0.6 * math.exp(-0.3 * i)
    lam = (jnp.exp(jnp.sum(lq1[j] * lk1[j]).astype(F32))
           - jnp.exp(jnp.sum(lq2[j] * lk2[j]).astype(F32)) + lambda_init)
    return lam.astype(F32), lambda_init


def _prompt_trunk(x_prompt, bias_table, norm_g, w_in, w_out, lq1, lk1, lq2, lk2, subln_g, final_g):
    B, S, _ = x_prompt.shape
    depth = w_in.shape[0]
    assert S % TQ == 0 and S // TQ <= LANES
    tiles = _prompt_bias_tiles(bias_table)
    x = x_prompt.reshape(B * S, D_MODEL)
    og = None
    new_k, new_v = [], []
    for i in range(depth):
        outs = _proj(x, og, None if og is None else w_out[i - 1].astype(BF16),
                     norm_g[i], w_in[i].astype(BF16))
        if og is not None:
            x, outs = outs[0], outs[1:]
        q, k, v, gate, kb, vt, kmean = outs
        new_k.append(k.reshape(B, S, N_HEADS, HEAD_DIM))
        new_v.append(v.reshape(B, S, N_HEADS, HEAD_DIM))
        kind = i % N_MIXERS
        extra = None
        if kind == 2:
            lam, lambda_init = _diff_lambda(i, lq1, lk1, lq2, lk2)
            extra = (lam, lambda_init, subln_g[i // N_MIXERS])
        og = _prompt_attention(kind, q, kb, vt, gate, kmean, tiles, extra, B, S)
    y = _proj(x, og, w_out[depth - 1].astype(BF16), final_g, None)[0]
    return y.reshape(B, S, D_MODEL), jnp.stack(new_k), jnp.stack(new_v)


def _sample_trunk(x_sample, cache_k, cache_v, page_table, bias_table, norm_g, w_in, w_out,
                  lq1, lk1, lq2, lk2, subln_g, final_g):
    DB, ds, _ = x_sample.shape
    depth = w_in.shape[0]
    x = x_sample.reshape(DB * ds, D_MODEL)
    og = None
    new_k, new_v = [], []
    for i in range(depth):
        outs = _proj(x, og, None if og is None else w_out[i - 1].astype(BF16),
                     norm_g[i], w_in[i].astype(BF16))
        if og is not None:
            x, outs = outs[0], outs[1:]
        q, k, v, gate = outs[:4]
        new_k.append(k.reshape(DB, ds, N_HEADS, HEAD_DIM))
        new_v.append(v.reshape(DB, ds, N_HEADS, HEAD_DIM))
        kind = i % N_MIXERS
        extra = None
        if kind == 2:
            lam, lambda_init = _diff_lambda(i, lq1, lk1, lq2, lk2)
            extra = (lam, lambda_init, subln_g[i // N_MIXERS])
        og = _decode_attention(kind, i, q, k, v, gate, cache_k, cache_v, page_table, bias_table,
                               extra)
    y = _proj(x, og, w_out[depth - 1].astype(BF16), final_g, None)[0]
    return y.reshape(DB, ds, D_MODEL), jnp.stack(new_k), jnp.stack(new_v)


QP = 8

_BATCH_NT = (((2,), (2,)), ((0,), (0,)))
_BATCH_NN = (((2,), (1,)), ((0,), (0,)))


def _bdot_nt(a, b):
    return lax.dot_general(a, b, _BATCH_NT, preferred_element_type=F32)


def _bdot(a, b):
    return lax.dot_general(a, b, _BATCH_NN, preferred_element_type=F32)


def _page_pair(a_ref, b_ref):
    return jnp.concatenate([a_ref[...], b_ref[...]], axis=2).astype(BF16)


def _new_slabs(kn_ref, vn_ref):
    pad = jnp.zeros((N_HEADS, LANES - QP, HEAD_DIM), F32)
    return (jnp.concatenate([kn_ref[...], pad], axis=1).astype(BF16),
            jnp.concatenate([vn_ref[...], pad], axis=1).astype(BF16))


def _moba_decode_kernel(pt_ref, q_ref, kn_ref, vn_ref, gt_ref, ka_ref, kb_ref, va_ref, vb_ref,
                        bt_ref, bo_ref, o_ref, m_sc, l_sc, g_sc, acc_sc, *, nblk):
    t = pl.program_id(1)
    qb = (q_ref[...] * HEAD_DIM ** -0.5).astype(BF16)
    raw = _bdot(qb, _page_pair(ka_ref, kb_ref))
    s = raw + bt_ref[...]
    m = jnp.max(s, axis=2, keepdims=True)
    p = jnp.exp(s - m)
    m_sc[t] = m
    l_sc[t] = jnp.sum(p, axis=2, keepdims=True)
    acc_sc[t] = _bdot_nt(p.astype(BF16), _page_pair(va_ref, vb_ref))
    g_sc[t] = jnp.sum(raw, axis=2, keepdims=True) * (1.0 / MOBA_BLOCK)

    @pl.when(t == nblk - 1)
    def _():
        g = g_sc[...]
        n_iota = lax.broadcasted_iota(jnp.int32, g.shape, 0)
        sel = jnp.zeros(g.shape, jnp.bool_)
        for _ in range(MOBA_TOPK):
            mx = jnp.max(g, axis=0, keepdims=True)
            cand = (g == mx) & jnp.logical_not(sel)
            idx = jnp.min(jnp.where(cand, n_iota, nblk), axis=0, keepdims=True)
            pick = n_iota == idx
            sel = sel | pick
            g = jnp.where(pick, -jnp.inf, g)
        kn, vn = _new_slabs(kn_ref, vn_ref)
        so = _bdot_nt(qb, kn) + bo_ref[...]
        mo = jnp.max(so, axis=2, keepdims=True)
        po = jnp.exp(so - mo)
        lo = jnp.sum(po, axis=2, keepdims=True)
        acc_o = _bdot(po.astype(BF16), vn)
        mb = m_sc[...]
        mt = jnp.maximum(mo, jnp.max(jnp.where(sel, mb, NEG), axis=0))
        w = jnp.where(sel, jnp.exp(mb - mt), 0.0)
        wo = jnp.exp(mo - mt)
        lt = wo * lo + jnp.sum(w * l_sc[...], axis=0)
        o = wo * acc_o + jnp.sum(w * acc_sc[...], axis=0)
        o_ref[...] = (o / lt) * _silu(gt_ref[...])


def _stick_decode_kernel(pt_ref, q_ref, kn_ref, vn_ref, gt_ref, ka_ref, kb_ref, va_ref, vb_ref,
                         uo_ref, ub_ref, o_ref, c_sc, acc_sc, *, nblk):
    t = pl.program_id(1)
    qb = (q_ref[...] * HEAD_DIM ** -0.5).astype(BF16)

    def weights(z, c, u_ref, past):
        sp = jnp.maximum(z, 0.0) + jnp.log(1.0 + jnp.exp(-jnp.abs(z)))
        ln = -sp
        if past is not None:
            ln = jnp.where(past, ln, 0.0)
        ln2 = ln.reshape(N_HEADS * QP, ln.shape[2])
        hi = ln2.astype(BF16)
        lo = (ln2 - hi.astype(F32)).astype(BF16)
        suf = _dot(jnp.concatenate([hi, lo], axis=1), u_ref[...]).reshape(ln.shape)
        a = jnp.exp((z - sp) + suf + c)
        if past is not None:
            a = jnp.where(past, a, 0.0)
        return a.astype(BF16), jnp.sum(ln, axis=2, keepdims=True)

    @pl.when(t == 0)
    def _():
        kn, vn = _new_slabs(kn_ref, vn_ref)
        shape = (N_HEADS, QP, LANES)
        past = lax.broadcasted_iota(jnp.int32, shape, 2) < lax.broadcasted_iota(jnp.int32, shape, 1)
        a, dc = weights(_bdot_nt(qb, kn), 0.0, uo_ref, past)
        acc_sc[...] = _bdot(a, vn)
        c_sc[...] = dc

    a, dc = weights(_bdot(qb, _page_pair(ka_ref, kb_ref)), c_sc[...], ub_ref, None)
    acc_sc[...] += _bdot_nt(a, _page_pair(va_ref, vb_ref))
    c_sc[...] += dc

    @pl.when(t == nblk - 1)
    def _():
        o_ref[...] = acc_sc[...] * _silu(gt_ref[...])


def _diff_decode_kernel(pt_ref, lam_ref, q_ref, kn_ref, vn_ref, gt_ref, ka_ref, kb_ref, va_ref, vb_ref,
                        bt_ref, bo_ref, sg_ref, o_ref, m_sc, l_sc, acc_sc, *, nblk, lambda_init):
    t = pl.program_id(1)
    q = q_ref[...] * DIFF_QK_DIM ** -0.5
    lane = lax.broadcasted_iota(jnp.int32, q.shape, 2)
    first = lane < DIFF_QK_DIM
    qb = jnp.concatenate([jnp.where(first, q, 0.0), jnp.where(first, 0.0, q)],
                         axis=1).astype(BF16)
    both = lambda x: jnp.concatenate([x, x], axis=1)

    @pl.when(t == 0)
    def _():
        kn, vn = _new_slabs(kn_ref, vn_ref)
        so = _bdot_nt(qb, kn) + both(bo_ref[...])
        mo = jnp.max(so, axis=2, keepdims=True)
        po = jnp.exp(so - mo)
        m_sc[...] = mo
        l_sc[...] = jnp.sum(po, axis=2, keepdims=True)
        acc_sc[...] = _bdot(po.astype(BF16), vn)

    s = _bdot(qb, _page_pair(ka_ref, kb_ref)) + both(bt_ref[...])
    m_old = m_sc[...]
    m = jnp.maximum(m_old, jnp.max(s, axis=2, keepdims=True))
    alpha = jnp.exp(m_old - m)
    p = jnp.exp(s - m)
    m_sc[...] = m
    l_sc[...] = alpha * l_sc[...] + jnp.sum(p, axis=2, keepdims=True)
    acc_sc[...] = alpha * acc_sc[...] + _bdot_nt(p.astype(BF16), _page_pair(va_ref, vb_ref))

    @pl.when(t == nblk - 1)
    def _():
        o = acc_sc[...] / l_sc[...]
        o = o[:, :QP] - lam_ref[0] * o[:, QP:]
        ms = jnp.mean(o * o, axis=2, keepdims=True)
        o = (o * lax.rsqrt(ms + SUBLN_EPS)) * sg_ref[...] * (1.0 - lambda_init)
        o_ref[...] = o * _silu(gt_ref[...])


def _decode_bias_tables(bias_table, past, ds, nblk):
    qi = jnp.arange(ds)
    kk = jnp.arange(MOBA_BLOCK)
    dist = past + qi[None, :, None] - (jnp.arange(nblk)[:, None, None] * MOBA_BLOCK + kk[None, None, :])
    bt = jnp.transpose(_rel_bias(bias_table, dist), (1, 0, 2, 3))
    ki = jnp.arange(LANES)
    ok = (ki[None, :] <= qi[:, None]) & (ki[None, :] < ds)
    bo = jnp.where(ok[None], _rel_bias(bias_table, qi[:, None] - ki[None, :]), NEG)
    padq = lambda x: jnp.pad(x, [(0, 0)] * (x.ndim - 2) + [(0, QP - ds), (0, 0)])
    return padq(bt).astype(F32), padq(bo).astype(F32)


def _decode_attention(kind, layer, q, k, v, gate, cache_k, cache_v, page_table, bias_table, extra):
    DB, n_pages = page_table.shape
    page = cache_k.shape[2]
    ds = q.shape[0] // DB
    assert MOBA_BLOCK == 2 * page and n_pages % 2 == 0 and ds <= QP
    nblk = n_pages // 2
    past = n_pages * page
    pt = page_table.reshape(-1).astype(jnp.int32)

    def head_major(x):
        x = jnp.transpose(x.reshape(DB, ds, N_HEADS, HEAD_DIM), (0, 2, 1, 3))
        return jnp.pad(x, ((0, 0), (0, 0), (0, QP - ds), (0, 0)))

    reverse = kind == 1

    def page_spec(which):
        def index(b, t, pt_ref, *_):
            blk = (nblk - 1 - t) if reverse else t
            return (layer, pt_ref[b * n_pages + blk * 2 + which], 0, 0, 0)
        return pl.BlockSpec((None, None, N_HEADS, HEAD_DIM, page), index)

    keys_minor = lambda c: jnp.transpose(c, (0, 1, 3, 4, 2))
    cache_k, cache_v = keys_minor(cache_k), keys_minor(cache_v)

    tok_spec = pl.BlockSpec((None, N_HEADS, QP, HEAD_DIM), lambda b, t, *_: (b, 0, 0, 0))
    fixed = lambda shape: pl.BlockSpec(shape, lambda b, t, *_: (0,) * len(shape))
    blk_bias = pl.BlockSpec((None, N_HEADS, QP, MOBA_BLOCK), lambda b, t, *_: (t, 0, 0, 0))
    pages = [cache_k, cache_k, cache_v, cache_v]
    page_specs = [page_spec(0), page_spec(1), page_spec(0), page_spec(1)]
    toks = [head_major(q), head_major(k), head_major(v), head_major(gate)]
    tok_specs = [tok_spec] * 4
    vm = lambda *shape: pltpu.VMEM(shape, F32)
    if kind == 0:
        bt, bo = _decode_bias_tables(bias_table, past, ds, nblk)
        body = functools.partial(_moba_decode_kernel, nblk=nblk)
        args = [pt, *toks, *pages, bt, bo]
        in_specs = [*tok_specs, *page_specs, blk_bias, fixed((N_HEADS, QP, LANES))]
        scratch = [vm(nblk, N_HEADS, QP, 1), vm(nblk, N_HEADS, QP, 1), vm(nblk, N_HEADS, QP, 1),
                   vm(nblk, N_HEADS, QP, HEAD_DIM)]
        num_prefetch = 1
    elif kind == 1:
        def umat(n):
            r = jnp.arange(2 * n)[:, None] % n
            return (r > jnp.arange(n)[None, :]).astype(BF16)
        body = functools.partial(_stick_decode_kernel, nblk=nblk)
        args = [pt, *toks, *pages, umat(LANES), umat(MOBA_BLOCK)]
        in_specs = [*tok_specs, *page_specs, fixed((2 * LANES, LANES)),
                    fixed((2 * MOBA_BLOCK, MOBA_BLOCK))]
        scratch = [vm(N_HEADS, QP, 1), vm(N_HEADS, QP, HEAD_DIM)]
        num_prefetch = 1
    else:
        lam, lambda_init, subln_g = extra
        bt, bo = _decode_bias_tables(bias_table, past, ds, nblk)
        body = functools.partial(_diff_decode_kernel, nblk=nblk, lambda_init=lambda_init)
        args = [pt, lam.reshape(1), *toks, *pages, bt, bo, subln_g.reshape(1, 1, HEAD_DIM)]
        in_specs = [*tok_specs, *page_specs, blk_bias, fixed((N_HEADS, QP, LANES)),
                    fixed((1, 1, HEAD_DIM))]
        scratch = [vm(N_HEADS, 2 * QP, 1), vm(N_HEADS, 2 * QP, 1), vm(N_HEADS, 2 * QP, HEAD_DIM)]
        num_prefetch = 2
    o4 = pl.pallas_call(
        body,
        grid_spec=pltpu.PrefetchScalarGridSpec(
            num_scalar_prefetch=num_prefetch,
            grid=(DB, nblk),
            in_specs=in_specs,
            out_specs=tok_spec,
            scratch_shapes=scratch),
        out_shape=jax.ShapeDtypeStruct((DB, N_HEADS, QP, HEAD_DIM), F32),
        compiler_params=pltpu.CompilerParams(
            dimension_semantics=("arbitrary", "arbitrary"), vmem_limit_bytes=VMEM_LIMIT),
        name=("moba", "stick", "diff")[kind] + "_decode",
    )(*args)
    return jnp.transpose(o4[:, :, :ds], (0, 2, 1, 3)).reshape(DB * ds, D_MODEL)


def kernel(x_prompt, x_sample, cache_k, cache_v, page_table, bias_table, norm_g, w_in, w_out,
           diff_lambda_q1, diff_lambda_k1, diff_lambda_q2, diff_lambda_k2, diff_subln_g,
           final_norm_g):
    lams = (diff_lambda_q1, diff_lambda_k1, diff_lambda_q2, diff_lambda_k2)
    y_p, k_p, v_p = _prompt_trunk(x_prompt, bias_table, norm_g, w_in, w_out, *lams,
                                  diff_subln_g, final_norm_g)
    y_s, k_s, v_s = _sample_trunk(x_sample, cache_k, cache_v, page_table, bias_table, norm_g,
                                  w_in, w_out, *lams, diff_subln_g, final_norm_g)
    return (y_p, y_s, k_p, v_p, k_s, v_s)
```

```python
import functools
import math

import jax
import jax.numpy as jnp
from jax import lax
from jax.experimental import pallas as pl
from jax.experimental.pallas import tpu as pltpu

N_HEADS = 16
HEAD_DIM = 64
D_MODEL = N_HEADS * HEAD_DIM
DIFF_QK_DIM = HEAD_DIM // 2
N_MIXERS = 3
MOBA_BLOCK = 256
MOBA_TOPK = 3
N_BUCKETS = 32
MAX_DISTANCE = 128
RMS_EPS = 1e-6
SUBLN_EPS = 1e-5

LANES = 128
HEADS_PER_TILE = LANES // HEAD_DIM
N_PAIRS = N_HEADS // HEADS_PER_TILE
TQ = MOBA_BLOCK
NEG = -1e30
LOG2E = math.log2(math.e)
EXIT_LOG2 = -160.0
VMEM_LIMIT = 48 * 1024 * 1024
F32 = jnp.float32
BF16 = jnp.bfloat16

def _dot(a, b):
    return jnp.dot(a, b, preferred_element_type=F32)


def _silu(x):
    return x / (1.0 + jnp.exp(-x))


def _proj_kernel(*refs, has_out, has_in, tm):
    it = iter(refs)
    x_ref = next(it)
    og_ref = next(it) if has_out else None
    wo_ref = next(it) if has_out else None
    g_ref = next(it)
    wi_ref = next(it) if has_in else None
    if has_out and has_in:
        xn_ref = next(it)
    if has_in:
        q_ref, k_ref, v_ref, gt_ref, kb_ref, vt_ref, km_ref = (next(it) for _ in range(7))
    else:
        y_ref = next(it)

    x = x_ref[...]
    if has_out:
        x = x + _dot(og_ref[...].astype(BF16), wo_ref[...])
        if has_in:
            xn_ref[...] = x
    r = lax.rsqrt(jnp.mean(x * x, axis=-1, keepdims=True) + RMS_EPS)
    h = (x * r) * g_ref[...]
    if not has_in:
        y_ref[...] = h
        return
    acc = _dot(h.astype(BF16), wi_ref[...])
    q_ref[...] = acc[:, 0 * D_MODEL:1 * D_MODEL]
    k = acc[:, 1 * D_MODEL:2 * D_MODEL]
    v = acc[:, 2 * D_MODEL:3 * D_MODEL]
    k_ref[...] = k
    v_ref[...] = v
    gt_ref[...] = acc[:, 3 * D_MODEL:4 * D_MODEL]
    kb_ref[...] = k.astype(BF16)
    nblk = tm // MOBA_BLOCK
    if nblk:
        km_ref[...] = jnp.sum(k.reshape(nblk, MOBA_BLOCK, D_MODEL), axis=1) * (1.0 / MOBA_BLOCK)
        row = lax.broadcasted_iota(jnp.int32, (N_PAIRS, LANES, MOBA_BLOCK), 1)
        for jb in range(nblk):
            vt = v[jb * MOBA_BLOCK:(jb + 1) * MOBA_BLOCK, :].T.reshape(N_PAIRS, LANES, MOBA_BLOCK)
            vt_ref[jb] = jnp.concatenate(
                [jnp.where(row < HEAD_DIM, vt, 0.0), jnp.where(row >= HEAD_DIM, vt, 0.0)],
                axis=2).astype(BF16)
    else:
        km_ref[...] = jnp.zeros(km_ref.shape, F32)
        vt_ref[...] = jnp.zeros(vt_ref.shape, BF16)


def _proj(x, og, w_out, g, w_in):
    T = x.shape[0]
    tm = min(512, T)
    assert T % tm == 0
    has_out, has_in = og is not None, w_in is not None
    row = lambda i: (i, 0)
    fixed = lambda i: (0, 0)
    args, in_specs = [x], [pl.BlockSpec((tm, D_MODEL), row)]
    if has_out:
        args += [og, w_out]
        in_specs += [pl.BlockSpec((tm, D_MODEL), row), pl.BlockSpec((D_MODEL, D_MODEL), fixed)]
    args.append(g.reshape(1, D_MODEL))
    in_specs.append(pl.BlockSpec((1, D_MODEL), fixed))
    out_shape, out_specs = [], []
    if has_in:
        args.append(w_in)
        in_specs.append(pl.BlockSpec((D_MODEL, 4 * D_MODEL), fixed))
        if has_out:
            out_shape.append(jax.ShapeDtypeStruct((T, D_MODEL), F32))
            out_specs.append(pl.BlockSpec((tm, D_MODEL), row))
        for dt in (F32, F32, F32, F32, BF16):
            out_shape.append(jax.ShapeDtypeStruct((T, D_MODEL), dt))
            out_specs.append(pl.BlockSpec((tm, D_MODEL), row))
        nblk = max(tm // MOBA_BLOCK, 1)
        out_shape.append(jax.ShapeDtypeStruct(
            (T // tm * nblk, N_PAIRS, LANES, 2 * MOBA_BLOCK), BF16))
        out_specs.append(pl.BlockSpec((nblk, N_PAIRS, LANES, 2 * MOBA_BLOCK),
                                      lambda i: (i, 0, 0, 0)))
        out_shape.append(jax.ShapeDtypeStruct((T // tm, nblk, D_MODEL), F32))
        out_specs.append(pl.BlockSpec((None, nblk, D_MODEL), lambda i: (i, 0, 0)))
    else:
        out_shape.append(jax.ShapeDtypeStruct((T, D_MODEL), F32))
        out_specs.append(pl.BlockSpec((tm, D_MODEL), row))
    outs = pl.pallas_call(
        functools.partial(_proj_kernel, has_out=has_out, has_in=has_in, tm=tm),
        grid=(T // tm,),
        in_specs=in_specs,
        out_specs=out_specs,
        out_shape=out_shape,
        compiler_params=pltpu.CompilerParams(
            dimension_semantics=("arbitrary",), vmem_limit_bytes=VMEM_LIMIT),
        name=f"proj_o{int(has_out)}_i{int(has_in)}",
    )(*args)
    return outs


def _t5_bucket(n):
    n = jnp.maximum(n, 0)
    max_exact = N_BUCKETS // 2
    nf = jnp.maximum(n, 1).astype(F32)
    large = max_exact + (jnp.log(nf / max_exact) / math.log(MAX_DISTANCE / max_exact)
                         * (N_BUCKETS - max_exact)).astype(jnp.int32)
    large = jnp.minimum(large, N_BUCKETS - 1)
    return jnp.where(n < max_exact, n, large)


def _rel_bias(bias_table, dist):
    b = bias_table[_t5_bucket(dist)] - bias_table[N_BUCKETS - 1]
    return jnp.moveaxis(b, -1, 0)


def _prompt_bias_tiles(bias_table):
    ki = jnp.arange(TQ)[:, None]
    qi = jnp.arange(TQ)[None, :]
    diag = jnp.where(ki <= qi, _rel_bias(bias_table, qi - ki) * LOG2E, NEG)
    adj = _rel_bias(bias_table, TQ + qi - ki) * LOG2E
    return diag.astype(F32), adj.astype(F32)


def _head_lane_masks(shape):
    lane = lax.broadcasted_iota(jnp.int32, shape, len(shape) - 1)
    return [(lane >= h * HEAD_DIM) & (lane < (h + 1) * HEAD_DIM) for h in range(HEADS_PER_TILE)]


def _row_masks():
    row = lax.broadcasted_iota(jnp.int32, (LANES, TQ), 0)
    return [(row >= h * HEAD_DIM) & (row < (h + 1) * HEAD_DIM) for h in range(HEADS_PER_TILE)]


def _pair_rows(x0, x1):
    return jnp.concatenate([jnp.broadcast_to(x0, (HEAD_DIM, TQ)),
                            jnp.broadcast_to(x1, (HEAD_DIM, TQ))], axis=0)


def _softmax_step_t(s, m, l):
    m_new = jnp.maximum(m, jnp.max(s, axis=0, keepdims=True))
    alpha = jnp.exp2(m - m_new)
    p = jnp.exp2(s - m_new)
    l = alpha * l + jnp.sum(p, axis=0, keepdims=True)
    return m_new, l, alpha, p.astype(BF16)


def _key_block(k_ref, j):
    return k_ref[pl.ds(pl.multiple_of(j * TQ, TQ), TQ), :]


PAIRS_PER_TRIP = 2


def _run_pipelined(scores, update, buf_a, buf_b, carry, n):
    last = jnp.maximum(n - 1, 0)

    def pair(t, c):
        j = 2 * t
        scores(j + 1, buf_b)
        c = update(buf_a, j, c)
        scores(jnp.minimum(j + 2, last), buf_a)
        return update(buf_b, j + 1, c)

    def pairs(t, c):
        for u in range(PAIRS_PER_TRIP):
            c = pair(t * PAIRS_PER_TRIP + u, c)
        return c

    n_pairs = n // 2
    n_trips = n_pairs // PAIRS_PER_TRIP
    carry = lax.fori_loop(0, n_trips, pairs, carry)
    carry = lax.fori_loop(n_trips * PAIRS_PER_TRIP, n_pairs, pair, carry)
    tail = update(buf_a, last, carry)
    return jax.tree_util.tree_map(lambda a, b: jnp.where(n % 2 == 1, a, b), tail, carry)


def _moba_kernel(q_ref, k_ref, vt_ref, km_ref, gt_ref, td_ref, ta_ref, o_ref, sa_ref, sb_ref, *,
                 nbp):
    i = pl.program_id(2)
    qT = q_ref[...].T
    rmask = _row_masks()
    n_iota = lax.broadcasted_iota(jnp.int32, (nbp, TQ), 0)
    valid = n_iota < i
    lane = lax.broadcasted_iota(jnp.int32, (TQ, LANES), 1)
    zeros_b = jnp.zeros((TQ, LANES), BF16)

    q_aug = []
    for h in range(HEADS_PER_TILE):
        qm = jnp.where(rmask[h], qT, 0.0)
        g = jnp.dot(km_ref[...], qm, preferred_element_type=F32,
                    precision=lax.Precision.HIGHEST)
        g = jnp.where(valid, g, -jnp.inf)
        sel = jnp.zeros(g.shape, jnp.bool_)
        for _ in range(MOBA_TOPK):
            mx = jnp.max(g, axis=0, keepdims=True)
            cand = (g == mx) & valid & jnp.logical_not(sel)
            idx = jnp.min(jnp.where(cand, n_iota, nbp), axis=0, keepdims=True)
            pick = n_iota == idx
            sel = sel | pick
            g = jnp.where(pick, -jnp.inf, g)
        selb = jnp.where(sel, 0.0, NEG)
        q_aug.append(jnp.concatenate(
            [qm * (HEAD_DIM ** -0.5 * LOG2E), selb, jnp.zeros((LANES - nbp, TQ), F32)],
            axis=0).astype(BF16))

    def scores(j, buf, onehot=True):
        ej = jnp.where(lane == j, 1.0, 0.0).astype(BF16) if onehot else zeros_b
        k_aug = jnp.concatenate([_key_block(k_ref, j), ej], axis=1)
        for h in range(HEADS_PER_TILE):
            buf[h] = _dot(k_aug, q_aug[h])

    def update(buf, j, carry, bias=None):
        stats, acc = carry
        new, alphas, ps = [], [], []
        for h in range(HEADS_PER_TILE):
            s = buf[h] if bias is None else buf[h] + bias[h]
            m, l, alpha, p = _softmax_step_t(s, *stats[h])
            new.append((m, l))
            alphas.append(alpha)
            ps.append(p)
        acc = acc * _pair_rows(*alphas) + _dot(vt_ref[j], jnp.concatenate(ps, axis=0))
        return tuple(new), acc

    init = (tuple((jnp.full((1, TQ), NEG, F32), jnp.zeros((1, TQ), F32))
                  for _ in range(HEADS_PER_TILE)), jnp.zeros((LANES, TQ), F32))
    prev = jnp.maximum(i - 1, 0)
    scores(i, sa_ref, onehot=False)
    scores(prev, sb_ref)
    carry = update(sa_ref, i, init, td_ref)
    scores(0, sa_ref)
    carry = update(sb_ref, prev, carry, ta_ref)
    stats, acc = _run_pipelined(scores, update, sa_ref, sb_ref, carry, jnp.maximum(i - 1, 0))
    o = (acc * _pair_rows(1.0 / stats[0][1], 1.0 / stats[1][1])).T
    o_ref[...] = (o * _silu(gt_ref[...])).astype(BF16)


def _stick_kernel(q_ref, k_ref, vt_ref, gt_ref, u_ref, o_ref):
    i = pl.program_id(2)
    qT = q_ref[...].T
    rmask = _row_masks()
    qs = [(jnp.where(rmask[h], qT, 0.0) * (HEAD_DIM ** -0.5 * LOG2E)).astype(BF16)
          for h in range(HEADS_PER_TILE)]
    ki = lax.broadcasted_iota(jnp.int32, (TQ, TQ), 0)
    qi = lax.broadcasted_iota(jnp.int32, (TQ, TQ), 1)
    past = ki < qi

    def step(j, carry, diag):
        cs, acc = carry
        kj = _key_block(k_ref, j)
        new_c, ws = [], []
        for h in range(HEADS_PER_TILE):
            z = _dot(kj, qs[h])
            sp = jnp.maximum(z, 0.0) + jnp.log2(1.0 + jnp.exp2(-jnp.abs(z)))
            ln = -sp
            if diag:
                ln = jnp.where(past, ln, 0.0)
            hi = ln.astype(BF16)
            lo = (ln - hi.astype(F32)).astype(BF16)
            suf = _dot(u_ref[...], jnp.concatenate([hi, lo], axis=0))
            a = jnp.exp2((z - sp) + suf + cs[h])
            if diag:
                a = jnp.where(past, a, 0.0)
            ws.append(a.astype(BF16))
            new_c.append(cs[h] + jnp.sum(ln, axis=0, keepdims=True))
        acc = acc + _dot(vt_ref[j], jnp.concatenate(ws, axis=0))
        return tuple(new_c), acc

    def left(cs):
        return jnp.max(jnp.maximum(cs[0], cs[1]))

    init = (tuple(jnp.zeros((1, TQ), F32) for _ in range(HEADS_PER_TILE)),
            jnp.zeros((LANES, TQ), F32))
    carry = step(i, init, True)

    def cond(state):
        t, cmax, _ = state
        return (t < i) & (cmax > EXIT_LOG2)

    def body(state):
        t, _, c = state
        c = step(i - 1 - t, c, False)
        return t + 1, left(c[0]), c

    _, _, carry = lax.while_loop(cond, body, (jnp.int32(0), left(carry[0]), carry))
    o_ref[...] = (carry[1].T * _silu(gt_ref[...])).astype(BF16)


def _diff_finish(o, hmask, sg_ref, lambda_init):
    o2 = o * o
    r = jnp.zeros(o.shape, F32)
    for h in range(HEADS_PER_TILE):
        ms = jnp.sum(jnp.where(hmask[h], o2, 0.0), axis=1, keepdims=True) * (1.0 / HEAD_DIM)
        r = jnp.where(hmask[h], lax.rsqrt(ms + SUBLN_EPS), r)
    return (o * r) * sg_ref[...] * (1.0 - lambda_init)


def _diff_kernel(lam_ref, q_ref, k_ref, vt_ref, gt_ref, td_ref, ta_ref, sg_ref, o_ref,
                 sa_ref, sb_ref, *, lambda_init):
    i = pl.program_id(2)
    qT = q_ref[...].T
    row = lax.broadcasted_iota(jnp.int32, qT.shape, 0)
    qm = [[None, None] for _ in range(HEADS_PER_TILE)]
    for h in range(HEADS_PER_TILE):
        for c in range(2):
            lo = h * HEAD_DIM + c * DIFF_QK_DIM
            msk = (row >= lo) & (row < lo + DIFF_QK_DIM)
            qm[h][c] = (jnp.where(msk, qT, 0.0) * (DIFF_QK_DIM ** -0.5 * LOG2E)).astype(BF16)

    def scores(j, buf):
        kj = _key_block(k_ref, j)
        for c in range(2):
            for h in range(HEADS_PER_TILE):
                buf[c * HEADS_PER_TILE + h] = _dot(kj, qm[h][c])

    def update(buf, j, carry, bias=None):
        vt = vt_ref[j]
        out = []
        for c in range(2):
            stats, acc = carry[c]
            new, alphas, ps = [], [], []
            for h in range(HEADS_PER_TILE):
                s = buf[c * HEADS_PER_TILE + h]
                if bias is not None:
                    s = s + bias[h]
                m, l, alpha, p = _softmax_step_t(s, *stats[h])
                new.append((m, l))
                alphas.append(alpha)
                ps.append(p)
            acc = acc * _pair_rows(*alphas) + _dot(vt, jnp.concatenate(ps, axis=0))
            out.append((tuple(new), acc))
        return tuple(out)

    init = tuple((tuple((jnp.full((1, TQ), NEG, F32), jnp.zeros((1, TQ), F32))
                        for _ in range(HEADS_PER_TILE)), jnp.zeros((LANES, TQ), F32))
                 for _ in range(2))
    prev = jnp.maximum(i - 1, 0)
    scores(i, sa_ref)
    scores(prev, sb_ref)
    carry = update(sa_ref, i, init, td_ref)
    scores(0, sa_ref)
    adj = update(sb_ref, prev, carry, ta_ref)
    carry = jax.tree_util.tree_map(lambda n, o: jnp.where(i > 0, n, o), adj, carry)
    carry = _run_pipelined(scores, update, sa_ref, sb_ref, carry, jnp.maximum(i - 1, 0))
    outs =[acc * _pair_rows(1.0 / stats[0][1], 1.0 / stats[1][1]) for stats, acc in carry]
    o = (outs[0] - lam_ref[0] * outs[1]).T
    o = _diff_finish(o, _head_lane_masks(o.shape), sg_ref, lambda_init)
    o_ref[...] = (o * _silu(gt_ref[...])).astype(BF16)


def _prompt_attention(kind, q, kb, vt, gate, kmean, tiles, extra, B, S):
    nq = S // TQ
    q3, g3 = q.reshape(B, S, D_MODEL), gate.reshape(B, S, D_MODEL)
    k3 = kb.reshape(B, S, D_MODEL)
    v3 = vt.reshape(B, nq, N_PAIRS, LANES, 2 * TQ)
    tile_spec = pl.BlockSpec((None, TQ, LANES), lambda b, p, i, *_: (b, i, p))
    seq_spec = pl.BlockSpec((None, S, LANES), lambda b, p, i, *_: (b, 0, p))
    vt_spec = pl.BlockSpec((None, nq, None, LANES, 2 * TQ), lambda b, p, i, *_: (b, 0, p, 0, 0))
    bias_spec = pl.BlockSpec((HEADS_PER_TILE, TQ, TQ), lambda b, p, i, *_: (p, 0, 0))
    td, ta = tiles
    num_prefetch = 0
    n_tiles = {0: HEADS_PER_TILE, 1: 0, 2: 2 * HEADS_PER_TILE}[kind]
    scratch = [pltpu.VMEM((n_tiles, TQ, TQ), F32)] * 2 if n_tiles else []
    if kind == 0:
        nbp = -(-nq // 8) * 8
        km = jnp.pad(kmean.reshape(B, nq, D_MODEL), ((0, 0), (0, nbp - nq), (0, 0)))
        body = functools.partial(_moba_kernel, nbp=nbp)
        args = [q3, k3, v3, km, g3, td, ta]
        in_specs = [tile_spec, seq_spec, vt_spec,
                    pl.BlockSpec((None, nbp, LANES), lambda b, p, i: (b, 0, p)),
                    tile_spec, bias_spec, bias_spec]
    elif kind == 1:
        c = jnp.arange(2 * TQ)[None, :] % TQ
        u = (c > jnp.arange(TQ)[:, None]).astype(BF16)
        body = _stick_kernel
        args = [q3, k3, v3, g3, u]
        in_specs = [tile_spec, seq_spec, vt_spec, tile_spec,
                    pl.BlockSpec((TQ, 2 * TQ), lambda b, p, i: (0, 0))]
    else:
        lam, lambda_init, subln_g = extra
        sg = jnp.tile(subln_g, HEADS_PER_TILE).reshape(1, LANES)
        body = functools.partial(_diff_kernel, lambda_init=lambda_init)
        args = [lam.reshape(1), q3, k3, v3, g3, td, ta, sg]
        num_prefetch = 1
        in_specs = [tile_spec, seq_spec, vt_spec, tile_spec, bias_spec, bias_spec,
                    pl.BlockSpec((1, LANES), lambda b, p, i, *_: (0, 0))]
    og = pl.pallas_call(
        body,
        grid_spec=pltpu.PrefetchScalarGridSpec(
            num_scalar_prefetch=num_prefetch,
            grid=(B, N_PAIRS, nq),
            in_specs=in_specs,
            out_specs=tile_spec,
            scratch_shapes=scratch),
        out_shape=jax.ShapeDtypeStruct((B, S, D_MODEL), BF16),
        compiler_params=pltpu.CompilerParams(
            dimension_semantics=("arbitrary", "arbitrary", "arbitrary"),
            vmem_limit_bytes=VMEM_LIMIT),
        name=("moba", "stick", "diff")[kind] + "_prompt",
    )(*args)
    return og.reshape(B * S, D_MODEL)


def _diff_lambda(i, lq1, lk1, lq2, lk2):
    j = i // N_MIXERS
    lambda_init = 0.8 - 0.6 * math.exp(-0.3 * i)
    lam = (jnp.exp(jnp.sum(lq1[j] * lk1[j]).astype(F32))
           - jnp.exp(jnp.sum(lq2[j] * lk2[j]).astype(F32)) + lambda_init)
    return lam.astype(F32), lambda_init


def _prompt_trunk(x_prompt, bias_table, norm_g, w_in, w_out, lq1, lk1, lq2, lk2, subln_g, final_g):
    B, S, _ = x_prompt.shape
    depth = w_in.shape[0]
    assert S % TQ == 0 and S // TQ <= LANES
    tiles = _prompt_bias_tiles(bias_table)
    x = x_prompt.reshape(B * S, D_MODEL)
    og = None
    new_k, new_v = [], []
    for i in range(depth):
        outs = _proj(x, og, None if og is None else w_out[i - 1].astype(BF16),
                     norm_g[i], w_in[i].astype(BF16))
        if og is not None:
            x, outs = outs[0], outs[1:]
        q, k, v, gate, kb, vt, kmean = outs
        new_k.append(k.reshape(B, S, N_HEADS, HEAD_DIM))
        new_v.append(v.reshape(B, S, N_HEADS, HEAD_DIM))
        kind = i % N_MIXERS
        extra = None
        if kind == 2:
            lam, lambda_init = _diff_lambda(i, lq1, lk1, lq2, lk2)
            extra = (lam, lambda_init, subln_g[i // N_MIXERS])
        og = _prompt_attention(kind, q, kb, vt, gate, kmean, tiles, extra, B, S)
    y = _proj(x, og, w_out[depth - 1].astype(BF16), final_g, None)[0]
    return y.reshape(B, S, D_MODEL), jnp.stack(new_k), jnp.stack(new_v)


def _sample_trunk(x_sample, cache_k, cache_v, page_table, bias_table, norm_g, w_in, w_out,
                  lq1, lk1, lq2, lk2, subln_g, final_g):
    DB, ds, _ = x_sample.shape
    depth = w_in.shape[0]
    x = x_sample.reshape(DB * ds, D_MODEL)
    og = None
    new_k, new_v = [], []
    for i in range(depth):
        outs = _proj(x, og, None if og is None else w_out[i - 1].astype(BF16),
                     norm_g[i], w_in[i].astype(BF16))
        if og is not None:
            x, outs = outs[0], outs[1:]
        q, k, v, gate = outs[:4]
        new_k.append(k.reshape(DB, ds, N_HEADS, HEAD_DIM))
        new_v.append(v.reshape(DB, ds, N_HEADS, HEAD_DIM))
        kind = i % N_MIXERS
        extra = None
        if kind == 2:
            lam, lambda_init = _diff_lambda(i, lq1, lk1, lq2, lk2)
            extra = (lam, lambda_init, subln_g[i // N_MIXERS])
        og = _decode_attention(kind, i, q, k, v, gate, cache_k, cache_v, page_table, bias_table,
                               extra)
    y = _proj(x, og, w_out[depth - 1].astype(BF16), final_g, None)[0]
    return y.reshape(DB, ds, D_MODEL), jnp.stack(new_k), jnp.stack(new_v)


QP = 8
BLOCKS_PER_STEP = 4

_BATCH_NT = (((2,), (2,)), ((0,), (0,)))
_BATCH_NN = (((2,), (1,)), ((0,), (0,)))


def _bdot_nt(a, b):
    return lax.dot_general(a, b, _BATCH_NT, preferred_element_type=F32)


def _bdot(a, b):
    return lax.dot_general(a, b, _BATCH_NN, preferred_element_type=F32)


def _page_pair(a_ref, b_ref):
    return jnp.concatenate([a_ref[...], b_ref[...]], axis=2).astype(BF16)


def _new_slabs(kn_ref, vn_ref):
    pad = jnp.zeros((N_HEADS, LANES - QP, HEAD_DIM), F32)
    return (jnp.concatenate([kn_ref[...], pad], axis=1).astype(BF16),
            jnp.concatenate([vn_ref[...], pad], axis=1).astype(BF16))


def _split_pages(refs):
    n = 2 * BLOCKS_PER_STEP
    pair = lambda rs: [rs[2 * u:2 * u + 2] for u in range(BLOCKS_PER_STEP)]
    return pair(refs[:n]), pair(refs[n:2 * n]), refs[2 * n:]


def _moba_decode_kernel(pt_ref, q_ref, kn_ref, vn_ref, gt_ref, *refs, nblk):
    kp, vp, (bt_ref, bo_ref, o_ref, m_sc, l_sc, g_sc, acc_sc) = _split_pages(refs)
    t = pl.program_id(1)
    qb = (q_ref[...] * HEAD_DIM ** -0.5).astype(BF16)
    raws = [_bdot(qb, _page_pair(*kp[u])) for u in range(BLOCKS_PER_STEP)]
    for u, raw in enumerate(raws):
        n = t * BLOCKS_PER_STEP + u
        s = raw + bt_ref[u]
        m = jnp.max(s, axis=2, keepdims=True)
        p = jnp.exp(s - m)
        m_sc[n] = m
        l_sc[n] = jnp.sum(p, axis=2, keepdims=True)
        acc_sc[n] = _bdot_nt(p.astype(BF16), _page_pair(*vp[u]))
        g_sc[n] = jnp.sum(raw, axis=2, keepdims=True) * (1.0 / MOBA_BLOCK)

    @pl.when(t == nblk // BLOCKS_PER_STEP - 1)
    def _():
        g = g_sc[...]
        n_iota = lax.broadcasted_iota(jnp.int32, g.shape, 0)
        sel = jnp.zeros(g.shape, jnp.bool_)
        for _ in range(MOBA_TOPK):
            mx = jnp.max(g, axis=0, keepdims=True)
            cand = (g == mx) & jnp.logical_not(sel)
            idx = jnp.min(jnp.where(cand, n_iota, nblk), axis=0, keepdims=True)
            pick = n_iota == idx
            sel = sel | pick
            g = jnp.where(pick, -jnp.inf, g)
        kn, vn = _new_slabs(kn_ref, vn_ref)
        so = _bdot_nt(qb, kn) + bo_ref[...]
        mo = jnp.max(so, axis=2, keepdims=True)
        po = jnp.exp(so - mo)
        lo = jnp.sum(po, axis=2, keepdims=True)
        acc_o = _bdot(po.astype(BF16), vn)
        mb = m_sc[...]
        mt = jnp.maximum(mo, jnp.max(jnp.where(sel, mb, NEG), axis=0))
        w = jnp.where(sel, jnp.exp(mb - mt), 0.0)
        wo = jnp.exp(mo - mt)
        lt = wo * lo + jnp.sum(w * l_sc[...], axis=0)
        o = wo * acc_o + jnp.sum(w * acc_sc[...], axis=0)
        o_ref[...] = (o / lt) * _silu(gt_ref[...])


def _stick_decode_kernel(pt_ref, q_ref, kn_ref, vn_ref, gt_ref, *refs, nblk):
    kp, vp, (uo_ref, ub_ref, o_ref, c_sc, acc_sc) = _split_pages(refs)
    t = pl.program_id(1)
    qb = (q_ref[...] * HEAD_DIM ** -0.5).astype(BF16)

    def log_weights(z, u_ref, past):
        sp = jnp.maximum(z, 0.0) + jnp.log(1.0 + jnp.exp(-jnp.abs(z)))
        ln = -sp
        if past is not None:
            ln = jnp.where(past, ln, 0.0)
        ln2 = ln.reshape(N_HEADS * QP, ln.shape[2])
        hi = ln2.astype(BF16)
        lo = (ln2 - hi.astype(F32)).astype(BF16)
        suf = _dot(jnp.concatenate([hi, lo], axis=1), u_ref[...]).reshape(ln.shape)
        return (z - sp) + suf, jnp.sum(ln, axis=2, keepdims=True)

    @pl.when(t == 0)
    def _():
        kn, vn = _new_slabs(kn_ref, vn_ref)
        shape = (N_HEADS, QP, LANES)
        past = lax.broadcasted_iota(jnp.int32, shape, 2) < lax.broadcasted_iota(jnp.int32, shape, 1)
        la, dc = log_weights(_bdot_nt(qb, kn), uo_ref, past)
        acc_sc[...] = _bdot(jnp.where(past, jnp.exp(la), 0.0).astype(BF16), vn)
        c_sc[...] = dc

    zs = [_bdot(qb, _page_pair(*kp[u])) for u in range(BLOCKS_PER_STEP)]
    parts = [log_weights(z, ub_ref, None) for z in zs]
    c, acc = c_sc[...], acc_sc[...]
    for u, (la, dc) in enumerate(parts):
        acc = acc + _bdot_nt(jnp.exp(la + c).astype(BF16), _page_pair(*vp[u]))
        c = c + dc
    c_sc[...] = c
    acc_sc[...] = acc

    @pl.when(t == nblk // BLOCKS_PER_STEP - 1)
    def _():
        o_ref[...] = acc_sc[...] * _silu(gt_ref[...])


def _diff_decode_kernel(pt_ref, lam_ref, q_ref, kn_ref, vn_ref, gt_ref, *refs, nblk, lambda_init):
    kp, vp, (bt_ref, bo_ref, sg_ref, o_ref, m_sc, l_sc, acc_sc) = _split_pages(refs)
    t = pl.program_id(1)
    q = q_ref[...] * DIFF_QK_DIM ** -0.5
    lane = lax.broadcasted_iota(jnp.int32, q.shape, 2)
    first = lane < DIFF_QK_DIM
    qb = jnp.concatenate([jnp.where(first, q, 0.0), jnp.where(first, 0.0, q)],
                         axis=1).astype(BF16)
    both = lambda x: jnp.concatenate([x, x], axis=1)

    @pl.when(t == 0)
    def _():
        kn, vn = _new_slabs(kn_ref, vn_ref)
        so = _bdot_nt(qb, kn) + both(bo_ref[...])
        mo = jnp.max(so, axis=2, keepdims=True)
        po = jnp.exp(so - mo)
        m_sc[...] = mo
        l_sc[...] = jnp.sum(po, axis=2, keepdims=True)
        acc_sc[...] = _bdot(po.astype(BF16), vn)

    along_keys = lambda xs: jnp.concatenate(xs, axis=2)
    s = _bdot(qb, along_keys([_page_pair(*kp[u]) for u in range(BLOCKS_PER_STEP)]))
    s = s + along_keys([both(bt_ref[u]) for u in range(BLOCKS_PER_STEP)])
    m_old = m_sc[...]
    m = jnp.maximum(m_old, jnp.max(s, axis=2, keepdims=True))
    alpha = jnp.exp(m_old - m)
    p = jnp.exp(s - m)
    m_sc[...] = m
    l_sc[...] = alpha * l_sc[...] + jnp.sum(p, axis=2, keepdims=True)
    acc_sc[...] = alpha * acc_sc[...] + _bdot_nt(
        p.astype(BF16), along_keys([_page_pair(*vp[u]) for u in range(BLOCKS_PER_STEP)]))

    @pl.when(t == nblk // BLOCKS_PER_STEP - 1)
    def _():
        o = acc_sc[...] / l_sc[...]
        o = o[:, :QP] - lam_ref[0] * o[:, QP:]
        ms = jnp.mean(o * o, axis=2, keepdims=True)
        o = (o * lax.rsqrt(ms + SUBLN_EPS)) * sg_ref[...] * (1.0 - lambda_init)
        o_ref[...] = o * _silu(gt_ref[...])


def _decode_bias_tables(bias_table, past, ds, nblk):
    qi = jnp.arange(ds)
    kk = jnp.arange(MOBA_BLOCK)
    dist = past + qi[None, :, None] - (jnp.arange(nblk)[:, None, None] * MOBA_BLOCK + kk[None, None, :])
    bt = jnp.transpose(_rel_bias(bias_table, dist), (1, 0, 2, 3))
    ki = jnp.arange(LANES)
    ok = (ki[None, :] <= qi[:, None]) & (ki[None, :] < ds)
    bo = jnp.where(ok[None], _rel_bias(bias_table, qi[:, None] - ki[None, :]), NEG)
    padq = lambda x: jnp.pad(x, [(0, 0)] * (x.ndim - 2) + [(0, QP - ds), (0, 0)])
    return padq(bt).astype(F32), padq(bo).astype(F32)


def _decode_attention(kind, layer, q, k, v, gate, cache_k, cache_v, page_table, bias_table, extra):
    DB, n_pages = page_table.shape
    page = cache_k.shape[2]
    ds = q.shape[0] // DB
    assert MOBA_BLOCK == 2 * page and n_pages % (2 * BLOCKS_PER_STEP) == 0 and ds <= QP
    nblk = n_pages // 2
    past = n_pages * page
    pt = page_table.reshape(-1).astype(jnp.int32)

    def head_major(x):
        x = jnp.transpose(x.reshape(DB, ds, N_HEADS, HEAD_DIM), (0, 2, 1, 3))
        return jnp.pad(x, ((0, 0), (0, 0), (0, QP - ds), (0, 0)))

    reverse = kind == 1

    def page_spec(u, which):
        def index(b, t, pt_ref, *_):
            blk = t * BLOCKS_PER_STEP + u
            if reverse:
                blk = nblk - 1 - blk
            return (layer, pt_ref[b * n_pages + blk * 2 + which], 0, 0, 0)
        return pl.BlockSpec((None, None, N_HEADS, HEAD_DIM, page), index)

    keys_minor = lambda c: jnp.transpose(c, (0, 1, 3, 4, 2))
    cache_k, cache_v = keys_minor(cache_k), keys_minor(cache_v)

    tok_spec = pl.BlockSpec((None, N_HEADS, QP, HEAD_DIM), lambda b, t, *_: (b, 0, 0, 0))
    fixed = lambda shape: pl.BlockSpec(shape, lambda b, t, *_: (0,) * len(shape))
    blk_bias = pl.BlockSpec((BLOCKS_PER_STEP, N_HEADS, QP, MOBA_BLOCK),
                            lambda b, t, *_: (t, 0, 0, 0))
    step_pages = [page_spec(u, w) for u in range(BLOCKS_PER_STEP) for w in range(2)]
    pages = [cache_k] * len(step_pages) + [cache_v] * len(step_pages)
    page_specs = step_pages + step_pages
    toks = [head_major(q), head_major(k), head_major(v), head_major(gate)]
    tok_specs = [tok_spec] * 4
    vm = lambda *shape: pltpu.VMEM(shape, F32)
    if kind == 0:
        bt, bo = _decode_bias_tables(bias_table, past, ds, nblk)
        body = functools.partial(_moba_decode_kernel, nblk=nblk)
        args = [pt, *toks, *pages, bt, bo]
        in_specs = [*tok_specs, *page_specs, blk_bias, fixed((N_HEADS, QP, LANES))]
        scratch = [vm(nblk, N_HEADS, QP, 1), vm(nblk, N_HEADS, QP, 1), vm(nblk, N_HEADS, QP, 1),
                   vm(nblk, N_HEADS, QP, HEAD_DIM)]
        num_prefetch = 1
    elif kind == 1:
        def umat(n):
            r = jnp.arange(2 * n)[:, None] % n
            return (r > jnp.arange(n)[None, :]).astype(BF16)
        body = functools.partial(_stick_decode_kernel, nblk=nblk)
        args = [pt, *toks, *pages, umat(LANES), umat(MOBA_BLOCK)]
        in_specs = [*tok_specs, *page_specs, fixed((2 * LANES, LANES)),
                    fixed((2 * MOBA_BLOCK, MOBA_BLOCK))]
        scratch = [vm(N_HEADS, QP, 1), vm(N_HEADS, QP, HEAD_DIM)]
        num_prefetch = 1
    else:
        lam, lambda_init, subln_g = extra
        bt, bo = _decode_bias_tables(bias_table, past, ds, nblk)
        body = functools.partial(_diff_decode_kernel, nblk=nblk, lambda_init=lambda_init)
        args = [pt, lam.reshape(1), *toks, *pages, bt, bo, subln_g.reshape(1, 1, HEAD_DIM)]
        in_specs = [*tok_specs, *page_specs, blk_bias, fixed((N_HEADS, QP, LANES)),
                    fixed((1, 1, HEAD_DIM))]
        scratch = [vm(N_HEADS, 2 * QP, 1), vm(N_HEADS, 2 * QP, 1), vm(N_HEADS, 2 * QP, HEAD_DIM)]
        num_prefetch = 2
    o4 = pl.pallas_call(
        body,
        grid_spec=pltpu.PrefetchScalarGridSpec(
            num_scalar_prefetch=num_prefetch,
            grid=(DB, nblk // BLOCKS_PER_STEP),
            in_specs=in_specs,
            out_specs=tok_spec,
            scratch_shapes=scratch),
        out_shape=jax.ShapeDtypeStruct((DB, N_HEADS, QP, HEAD_DIM), F32),
        compiler_params=pltpu.CompilerParams(
            dimension_semantics=("arbitrary", "arbitrary"), vmem_limit_bytes=VMEM_LIMIT),
        name=("moba", "stick", "diff")[kind] + "_decode",
    )(*args)
    return jnp.transpose(o4[:, :, :ds], (0, 2, 1, 3)).reshape(DB * ds, D_MODEL)


def kernel(x_prompt, x_sample, cache_k, cache_v, page_table, bias_table, norm_g, w_in, w_out,
           diff_lambda_q1, diff_lambda_k1, diff_lambda_q2, diff_lambda_k2, diff_subln_g,
           final_norm_g):
    lams = (diff_lambda_q1, diff_lambda_k1, diff_lambda_q2, diff_lambda_k2)
    y_p, k_p, v_p = _prompt_trunk(x_prompt, bias_table, norm_g, w_in, w_out, *lams,
                                  diff_subln_g, final_norm_g)
    y_s, k_s, v_s = _sample_trunk(x_sample, cache_k, cache_v, page_table, bias_table, norm_g,
                                  w_in, w_out, *lams, diff_subln_g, final_norm_g)
    return (y_p, y_s, k_p, v_p, k_s, v_s)
```

```python
import functools
import math

import jax
import jax.numpy as jnp
from jax import lax
from jax.experimental import pallas as pl
from jax.experimental.pallas import tpu as pltpu

N_HEADS = 16
HEAD_DIM = 64
D_MODEL = N_HEADS * HEAD_DIM
DIFF_QK_DIM = HEAD_DIM // 2
N_MIXERS = 3
MOBA_BLOCK = 256
MOBA_TOPK = 3
N_BUCKETS = 32
MAX_DISTANCE = 128
RMS_EPS = 1e-6
SUBLN_EPS = 1e-5

LANES = 128
HEADS_PER_TILE = LANES // HEAD_DIM
N_PAIRS = N_HEADS // HEADS_PER_TILE
TQ = MOBA_BLOCK
NEG = -1e30
LOG2E = math.log2(math.e)
EXIT_LOG2 = -160.0
VMEM_LIMIT = 48 * 1024 * 1024
F32 = jnp.float32
BF16 = jnp.bfloat16

def _dot(a, b):
    return jnp.dot(a, b, preferred_element_type=F32)


def _silu(x):
    return x / (1.0 + jnp.exp(-x))


def _proj_kernel(*refs, has_out, has_in, tm):
    it = iter(refs)
    x_ref = next(it)
    og_ref = next(it) if has_out else None
    wo_ref = next(it) if has_out else None
    g_ref = next(it)
    wi_ref = next(it) if has_in else None
    if has_out and has_in:
        xn_ref = next(it)
    if has_in:
        q_ref, k_ref, v_ref, gt_ref, kb_ref, vt_ref, km_ref = (next(it) for _ in range(7))
    else:
        y_ref = next(it)

    x = x_ref[...]
    if has_out:
        x = x + _dot(og_ref[...].astype(BF16), wo_ref[...])
        if has_in:
            xn_ref[...] = x
    r = lax.rsqrt(jnp.mean(x * x, axis=-1, keepdims=True) + RMS_EPS)
    h = (x * r) * g_ref[...]
    if not has_in:
        y_ref[...] = h
        return
    acc = _dot(h.astype(BF16), wi_ref[...])
    q_ref[...] = acc[:, 0 * D_MODEL:1 * D_MODEL]
    k = acc[:, 1 * D_MODEL:2 * D_MODEL]
    v = acc[:, 2 * D_MODEL:3 * D_MODEL]
    k_ref[...] = k
    v_ref[...] = v
    gt_ref[...] = acc[:, 3 * D_MODEL:4 * D_MODEL]
    kb_ref[...] = k.astype(BF16)
    nblk = tm // MOBA_BLOCK
    if nblk:
        km_ref[...] = jnp.sum(k.reshape(nblk, MOBA_BLOCK, D_MODEL), axis=1) * (1.0 / MOBA_BLOCK)
        row = lax.broadcasted_iota(jnp.int32, (N_PAIRS, LANES, MOBA_BLOCK), 1)
        for jb in range(nblk):
            vt = v[jb * MOBA_BLOCK:(jb + 1) * MOBA_BLOCK, :].T.reshape(N_PAIRS, LANES, MOBA_BLOCK)
            vt_ref[jb] = jnp.concatenate(
                [jnp.where(row < HEAD_DIM, vt, 0.0), jnp.where(row >= HEAD_DIM, vt, 0.0)],
                axis=2).astype(BF16)
    else:
        km_ref[...] = jnp.zeros(km_ref.shape, F32)
        vt_ref[...] = jnp.zeros(vt_ref.shape, BF16)


def _proj(x, og, w_out, g, w_in):
    T = x.shape[0]
    tm = min(512, T)
    assert T % tm == 0
    has_out, has_in = og is not None, w_in is not None
    row = lambda i: (i, 0)
    fixed = lambda i: (0, 0)
    args, in_specs = [x], [pl.BlockSpec((tm, D_MODEL), row)]
    if has_out:
        args += [og, w_out]
        in_specs += [pl.BlockSpec((tm, D_MODEL), row), pl.BlockSpec((D_MODEL, D_MODEL), fixed)]
    args.append(g.reshape(1, D_MODEL))
    in_specs.append(pl.BlockSpec((1, D_MODEL), fixed))
    out_shape, out_specs = [], []
    if has_in:
        args.append(w_in)
        in_specs.append(pl.BlockSpec((D_MODEL, 4 * D_MODEL), fixed))
        if has_out:
            out_shape.append(jax.ShapeDtypeStruct((T, D_MODEL), F32))
            out_specs.append(pl.BlockSpec((tm, D_MODEL), row))
        for dt in (F32, F32, F32, F32, BF16):
            out_shape.append(jax.ShapeDtypeStruct((T, D_MODEL), dt))
            out_specs.append(pl.BlockSpec((tm, D_MODEL), row))
        nblk = max(tm // MOBA_BLOCK, 1)
        out_shape.append(jax.ShapeDtypeStruct(
            (T // tm * nblk, N_PAIRS, LANES, 2 * MOBA_BLOCK), BF16))
        out_specs.append(pl.BlockSpec((nblk, N_PAIRS, LANES, 2 * MOBA_BLOCK),
                                      lambda i: (i, 0, 0, 0)))
        out_shape.append(jax.ShapeDtypeStruct((T // tm, nblk, D_MODEL), F32))
        out_specs.append(pl.BlockSpec((None, nblk, D_MODEL), lambda i: (i, 0, 0)))
    else:
        out_shape.append(jax.ShapeDtypeStruct((T, D_MODEL), F32))
        out_specs.append(pl.BlockSpec((tm, D_MODEL), row))
    outs = pl.pallas_call(
        functools.partial(_proj_kernel, has_out=has_out, has_in=has_in, tm=tm),
        grid=(T // tm,),
        in_specs=in_specs,
        out_specs=out_specs,
        out_shape=out_shape,
        compiler_params=pltpu.CompilerParams(
            dimension_semantics=("arbitrary",), vmem_limit_bytes=VMEM_LIMIT),
        name=f"proj_o{int(has_out)}_i{int(has_in)}",
    )(*args)
    return outs


def _t5_bucket(n):
    n = jnp.maximum(n, 0)
    max_exact = N_BUCKETS // 2
    nf = jnp.maximum(n, 1).astype(F32)
    large = max_exact + (jnp.log(nf / max_exact) / math.log(MAX_DISTANCE / max_exact)
                         * (N_BUCKETS - max_exact)).astype(jnp.int32)
    large = jnp.minimum(large, N_BUCKETS - 1)
    return jnp.where(n < max_exact, n, large)


def _rel_bias(bias_table, dist):
    onehot = (_t5_bucket(dist)[..., None] == jnp.arange(N_BUCKETS)).astype(F32)
    shifted = bias_table - bias_table[N_BUCKETS - 1]
    b = jnp.einsum('...n,nh->...h', onehot, shifted, precision=lax.Precision.HIGHEST)
    return jnp.moveaxis(b, -1, 0)


def _prompt_bias_tiles(bias_table):
    ki = jnp.arange(TQ)[:, None]
    qi = jnp.arange(TQ)[None, :]
    diag = jnp.where(ki <= qi, _rel_bias(bias_table, qi - ki) * LOG2E, NEG)
    adj = _rel_bias(bias_table, TQ + qi - ki) * LOG2E
    return diag.astype(F32), adj.astype(F32)


def _head_lane_masks(shape):
    lane = lax.broadcasted_iota(jnp.int32, shape, len(shape) - 1)
    return [(lane >= h * HEAD_DIM) & (lane < (h + 1) * HEAD_DIM) for h in range(HEADS_PER_TILE)]


def _row_masks():
    row = lax.broadcasted_iota(jnp.int32, (LANES, TQ), 0)
    return [(row >= h * HEAD_DIM) & (row < (h + 1) * HEAD_DIM) for h in range(HEADS_PER_TILE)]


def _pair_rows(x0, x1):
    return jnp.concatenate([jnp.broadcast_to(x0, (HEAD_DIM, TQ)),
                            jnp.broadcast_to(x1, (HEAD_DIM, TQ))], axis=0)


def _softmax_step_t(s, m, l):
    m_new = jnp.maximum(m, jnp.max(s, axis=0, keepdims=True))
    alpha = jnp.exp2(m - m_new)
    p = jnp.exp2(s - m_new)
    l = alpha * l + jnp.sum(p, axis=0, keepdims=True)
    return m_new, l, alpha, p.astype(BF16)


def _key_block(k_ref, j):
    return k_ref[pl.ds(pl.multiple_of(j * TQ, TQ), TQ), :]


PAIRS_PER_TRIP = 4


def _run_pipelined(scores, update, buf_a, buf_b, carry, n):
    last = jnp.maximum(n - 1, 0)

    def pair(t, c):
        j = 2 * t
        scores(j + 1, buf_b)
        c = update(buf_a, j, c)
        scores(jnp.minimum(j + 2, last), buf_a)
        return update(buf_b, j + 1, c)

    def pairs(t, c):
        for u in range(PAIRS_PER_TRIP):
            c = pair(t * PAIRS_PER_TRIP + u, c)
        return c

    n_pairs = n // 2
    n_trips = n_pairs // PAIRS_PER_TRIP
    carry = lax.fori_loop(0, n_trips, pairs, carry)
    carry = lax.fori_loop(n_trips * PAIRS_PER_TRIP, n_pairs, pair, carry)
    return lax.cond(n % 2 == 1, lambda c: update(buf_a, last, c), lambda c: c, carry)


def _moba_kernel(q_ref, k_ref, vt_ref, km_ref, gt_ref, td_ref, ta_ref, o_ref, sa_ref, sb_ref, *,
                 nbp):
    i = pl.program_id(2)
    qT = q_ref[...].T
    rmask = _row_masks()
    n_iota = lax.broadcasted_iota(jnp.int32, (nbp, TQ), 0)
    valid = n_iota < i
    lane = lax.broadcasted_iota(jnp.int32, (TQ, LANES), 1)
    zeros_b = jnp.zeros((TQ, LANES), BF16)

    q_aug = []
    for h in range(HEADS_PER_TILE):
        qm = jnp.where(rmask[h], qT, 0.0)
        g = jnp.dot(km_ref[...], qm, preferred_element_type=F32,
                    precision=lax.Precision.HIGHEST)
        g = jnp.where(valid, g, -jnp.inf)
        sel = jnp.zeros(g.shape, jnp.bool_)
        for _ in range(MOBA_TOPK):
            mx = jnp.max(g, axis=0, keepdims=True)
            cand = (g == mx) & valid & jnp.logical_not(sel)
            idx = jnp.min(jnp.where(cand, n_iota, nbp), axis=0, keepdims=True)
            pick = n_iota == idx
            sel = sel | pick
            g = jnp.where(pick, -jnp.inf, g)
        selb = jnp.where(sel, 0.0, NEG)
        q_aug.append(jnp.concatenate(
            [qm * (HEAD_DIM ** -0.5 * LOG2E), selb, jnp.zeros((LANES - nbp, TQ), F32)],
            axis=0).astype(BF16))

    def scores(j, buf, onehot=True):
        ej = jnp.where(lane == j, 1.0, 0.0).astype(BF16) if onehot else zeros_b
        k_aug = jnp.concatenate([_key_block(k_ref, j), ej], axis=1)
        for h in range(HEADS_PER_TILE):
            buf[h] = _dot(k_aug, q_aug[h])

    def update(buf, j, carry, bias=None):
        stats, acc = carry
        new, alphas, ps = [], [], []
        for h in range(HEADS_PER_TILE):
            s = buf[h] if bias is None else buf[h] + bias[h]
            m, l, alpha, p = _softmax_step_t(s, *stats[h])
            new.append((m, l))
            alphas.append(alpha)
            ps.append(p)
        acc = acc * _pair_rows(*alphas) + _dot(vt_ref[j], jnp.concatenate(ps, axis=0))
        return tuple(new), acc

    init = (tuple((jnp.full((1, TQ), NEG, F32), jnp.zeros((1, TQ), F32))
                  for _ in range(HEADS_PER_TILE)), jnp.zeros((LANES, TQ), F32))
    prev = jnp.maximum(i - 1, 0)
    scores(i, sa_ref, onehot=False)
    scores(prev, sb_ref)
    carry = update(sa_ref, i, init, td_ref)
    scores(0, sa_ref)
    carry = update(sb_ref, prev, carry, ta_ref)
    stats, acc = _run_pipelined(scores, update, sa_ref, sb_ref, carry, jnp.maximum(i - 1, 0))
    o = (acc * _pair_rows(1.0 / stats[0][1], 1.0 / stats[1][1])).T
    o_ref[...] = (o * _silu(gt_ref[...])).astype(BF16)


def _stick_kernel(q_ref, k_ref, vt_ref, gt_ref, u_ref, o_ref):
    i = pl.program_id(2)
    qT = q_ref[...].T
    rmask = _row_masks()
    qs = [(jnp.where(rmask[h], qT, 0.0) * (HEAD_DIM ** -0.5 * LOG2E)).astype(BF16)
          for h in range(HEADS_PER_TILE)]
    ki = lax.broadcasted_iota(jnp.int32, (TQ, TQ), 0)
    qi = lax.broadcasted_iota(jnp.int32, (TQ, TQ), 1)
    past = ki < qi

    def step(j, carry, diag):
        cs, acc = carry
        kj = _key_block(k_ref, j)
        new_c, ws = [], []
        for h in range(HEADS_PER_TILE):
            z = _dot(kj, qs[h])
            sp = jnp.maximum(z, 0.0) + jnp.log2(1.0 + jnp.exp2(-jnp.abs(z)))
            ln = -sp
            if diag:
                ln = jnp.where(past, ln, 0.0)
            hi = ln.astype(BF16)
            lo = (ln - hi.astype(F32)).astype(BF16)
            suf = _dot(u_ref[...], jnp.concatenate([hi, lo], axis=0))
            a = jnp.exp2((z - sp) + suf + cs[h])
            if diag:
                a = jnp.where(past, a, 0.0)
            ws.append(a.astype(BF16))
            new_c.append(cs[h] + jnp.sum(ln, axis=0, keepdims=True))
        acc = acc + _dot(vt_ref[j], jnp.concatenate(ws, axis=0))
        return tuple(new_c), acc

    def left(cs):
        return jnp.max(jnp.maximum(cs[0], cs[1]))

    init = (tuple(jnp.zeros((1, TQ), F32) for _ in range(HEADS_PER_TILE)),
            jnp.zeros((LANES, TQ), F32))
    carry = step(i, init, True)

    def cond(state):
        t, cmax, _ = state
        return (t < i) & (cmax > EXIT_LOG2)

    def body(state):
        t, _, c = state
        c = step(i - 1 - t, c, False)
        return t + 1, left(c[0]), c

    _, _, carry = lax.while_loop(cond, body, (jnp.int32(0), left(carry[0]), carry))
    o_ref[...] = (carry[1].T * _silu(gt_ref[...])).astype(BF16)


def _diff_finish(o, hmask, sg_ref, lambda_init):
    o2 = o * o
    r = jnp.zeros(o.shape, F32)
    for h in range(HEADS_PER_TILE):
        ms = jnp.sum(jnp.where(hmask[h], o2, 0.0), axis=1, keepdims=True) * (1.0 / HEAD_DIM)
        r = jnp.where(hmask[h], lax.rsqrt(ms + SUBLN_EPS), r)
    return (o * r) * sg_ref[...] * (1.0 - lambda_init)


def _diff_kernel(lam_ref, q_ref, k_ref, vt_ref, gt_ref, td_ref, ta_ref, sg_ref, o_ref,
                 sa_ref, sb_ref, *, lambda_init):
    i = pl.program_id(2)
    qT = q_ref[...].T
    row = lax.broadcasted_iota(jnp.int32, qT.shape, 0)
    qm = [[None, None] for _ in range(HEADS_PER_TILE)]
    for h in range(HEADS_PER_TILE):
        for c in range(2):
            lo = h * HEAD_DIM + c * DIFF_QK_DIM
            msk = (row >= lo) & (row < lo + DIFF_QK_DIM)
            qm[h][c] = (jnp.where(msk, qT, 0.0) * (DIFF_QK_DIM ** -0.5 * LOG2E)).astype(BF16)

    def scores(j, buf):
        kj = _key_block(k_ref, j)
        for c in range(2):
            for h in range(HEADS_PER_TILE):
                buf[c * HEADS_PER_TILE + h] = _dot(kj, qm[h][c])

    def update(buf, j, carry, bias=None):
        vt = vt_ref[j]
        out = []
        for c in range(2):
            stats, acc = carry[c]
            new, alphas, ps = [], [], []
            for h in range(HEADS_PER_TILE):
                s = buf[c * HEADS_PER_TILE + h]
                if bias is not None:
                    s = s + bias[h]
                m, l, alpha, p = _softmax_step_t(s, *stats[h])
                new.append((m, l))
                alphas.append(alpha)
                ps.append(p)
            acc = acc * _pair_rows(*alphas) + _dot(vt, jnp.concatenate(ps, axis=0))
            out.append((tuple(new), acc))
        return tuple(out)

    init = tuple((tuple((jnp.full((1, TQ), NEG, F32), jnp.zeros((1, TQ), F32))
                        for _ in range(HEADS_PER_TILE)), jnp.zeros((LANES, TQ), F32))
                 for _ in range(2))
    prev = jnp.maximum(i - 1, 0)
    scores(i, sa_ref)
    scores(prev, sb_ref)
    carry = update(sa_ref, i, init, td_ref)
    scores(0, sa_ref)
    adj = update(sb_ref, prev, carry, ta_ref)
    carry = jax.tree_util.tree_map(lambda n, o: jnp.where(i > 0, n, o), adj, carry)
    carry = _run_pipelined(scores, update, sa_ref, sb_ref, carry, jnp.maximum(i - 1, 0))
    outs =[acc * _pair_rows(1.0 / stats[0][1], 1.0 / stats[1][1]) for stats, acc in carry]
    o = (outs[0] - lam_ref[0] * outs[1]).T
    o = _diff_finish(o, _head_lane_masks(o.shape), sg_ref, lambda_init)
    o_ref[...] = (o * _silu(gt_ref[...])).astype(BF16)


def _prompt_attention(kind, q, kb, vt, gate, kmean, tiles, extra, B, S):
    nq = S // TQ
    q3, g3 = q.reshape(B, S, D_MODEL), gate.reshape(B, S, D_MODEL)
    k3 = kb.reshape(B, S, D_MODEL)
    v3 = vt.reshape(B, nq, N_PAIRS, LANES, 2 * TQ)
    tile_spec = pl.BlockSpec((None, TQ, LANES), lambda b, p, i, *_: (b, i, p))
    seq_spec = pl.BlockSpec((None, S, LANES), lambda b, p, i, *_: (b, 0, p))
    vt_spec = pl.BlockSpec((None, nq, None, LANES, 2 * TQ), lambda b, p, i, *_: (b, 0, p, 0, 0))
    bias_spec = pl.BlockSpec((HEADS_PER_TILE, TQ, TQ), lambda b, p, i, *_: (p, 0, 0))
    td, ta = tiles
    num_prefetch = 0
    n_tiles = {0: HEADS_PER_TILE, 1: 0, 2: 2 * HEADS_PER_TILE}[kind]
    scratch = [pltpu.VMEM((n_tiles, TQ, TQ), F32)] * 2 if n_tiles else []
    if kind == 0:
        nbp = -(-nq // 8) * 8
        km = jnp.pad(kmean.reshape(B, nq, D_MODEL), ((0, 0), (0, nbp - nq), (0, 0)))
        body = functools.partial(_moba_kernel, nbp=nbp)
        args = [q3, k3, v3, km, g3, td, ta]
        in_specs = [tile_spec, seq_spec, vt_spec,
                    pl.BlockSpec((None, nbp, LANES), lambda b, p, i: (b, 0, p)),
                    tile_spec, bias_spec, bias_spec]
    elif kind == 1:
        c = jnp.arange(2 * TQ)[None, :] % TQ
        u = (c > jnp.arange(TQ)[:, None]).astype(BF16)
        body = _stick_kernel
        args = [q3, k3, v3, g3, u]
        in_specs = [tile_spec, seq_spec, vt_spec, tile_spec,
                    pl.BlockSpec((TQ, 2 * TQ), lambda b, p, i: (0, 0))]
    else:
        lam, lambda_init, subln_g = extra
        sg = jnp.tile(subln_g, HEADS_PER_TILE).reshape(1, LANES)
        body = functools.partial(_diff_kernel, lambda_init=lambda_init)
        args = [lam.reshape(1), q3, k3, v3, g3, td, ta, sg]
        num_prefetch = 1
        in_specs = [tile_spec, seq_spec, vt_spec, tile_spec, bias_spec, bias_spec,
                    pl.BlockSpec((1, LANES), lambda b, p, i, *_: (0, 0))]
    og = pl.pallas_call(
        body,
        grid_spec=pltpu.PrefetchScalarGridSpec(
            num_scalar_prefetch=num_prefetch,
            grid=(B, N_PAIRS, nq),
            in_specs=in_specs,
            out_specs=tile_spec,
            scratch_shapes=scratch),
        out_shape=jax.ShapeDtypeStruct((B, S, D_MODEL), BF16),
        compiler_params=pltpu.CompilerParams(
            dimension_semantics=("arbitrary", "arbitrary", "arbitrary"),
            vmem_limit_bytes=VMEM_LIMIT),
        name=("moba", "stick", "diff")[kind] + "_prompt",
    )(*args)
    return og.reshape(B * S, D_MODEL)


def _diff_lambda(i, lq1, lk1, lq2, lk2):
    j = i // N_MIXERS
    lambda_init = 0.8 - 0.6 * math.exp(-0.3 * i)
    lam = (jnp.exp(jnp.sum(lq1[j] * lk1[j]).astype(F32))
           - jnp.exp(jnp.sum(lq2[j] * lk2[j]).astype(F32)) + lambda_init)
    return lam.astype(F32), lambda_init


def _prompt_trunk(x_prompt, bias_table, norm_g, w_in, w_out, lq1, lk1, lq2, lk2, subln_g, final_g):
    B, S, _ = x_prompt.shape
    depth = w_in.shape[0]
    assert S % TQ == 0 and S // TQ <= LANES
    tiles = _prompt_bias_tiles(bias_table)
    x = x_prompt.reshape(B * S, D_MODEL)
    og = None
    new_k, new_v = [], []
    for i in range(depth):
        outs = _proj(x, og, None if og is None else w_out[i - 1].astype(BF16),
                     norm_g[i], w_in[i].astype(BF16))
        if og is not None:
            x, outs = outs[0], outs[1:]
        q, k, v, gate, kb, vt, kmean = outs
        new_k.append(k.reshape(B, S, N_HEADS, HEAD_DIM))
        new_v.append(v.reshape(B, S, N_HEADS, HEAD_DIM))
        kind = i % N_MIXERS
        extra = None
        if kind == 2:
            lam, lambda_init = _diff_lambda(i, lq1, lk1, lq2, lk2)
            extra = (lam, lambda_init, subln_g[i // N_MIXERS])
        og = _prompt_attention(kind, q, kb, vt, gate, kmean, tiles, extra, B, S)
    y = _proj(x, og, w_out[depth - 1].astype(BF16), final_g, None)[0]
    return y.reshape(B, S, D_MODEL), jnp.stack(new_k), jnp.stack(new_v)


def _sample_trunk(x_sample, cache_k, cache_v, page_table, bias_table, norm_g, w_in, w_out,
                  lq1, lk1, lq2, lk2, subln_g, final_g):
    DB, ds, _ = x_sample.shape
    depth = w_in.shape[0]
    x = x_sample.reshape(DB * ds, D_MODEL)
    og = None
    new_k, new_v = [], []
    for i in range(depth):
        outs = _proj(x, og, None if og is None else w_out[i - 1].astype(BF16),
                     norm_g[i], w_in[i].astype(BF16))
        if og is not None:
            x, outs = outs[0], outs[1:]
        q, k, v, gate = outs[:4]
        new_k.append(k.reshape(DB, ds, N_HEADS, HEAD_DIM))
        new_v.append(v.reshape(DB, ds, N_HEADS, HEAD_DIM))
        kind = i % N_MIXERS
        extra = None
        if kind == 2:
            lam, lambda_init = _diff_lambda(i, lq1, lk1, lq2, lk2)
            extra = (lam, lambda_init, subln_g[i // N_MIXERS])
        og = _decode_attention(kind, i, q, k, v, gate, cache_k, cache_v, page_table, bias_table,
                               extra)
    y = _proj(x, og, w_out[depth - 1].astype(BF16), final_g, None)[0]
    return y.reshape(DB, ds, D_MODEL), jnp.stack(new_k), jnp.stack(new_v)


QP = 8
BLOCKS_PER_STEP = 4

_BATCH_NT = (((2,), (2,)), ((0,), (0,)))
_BATCH_NN = (((2,), (1,)), ((0,), (0,)))


def _bdot_nt(a, b):
    return lax.dot_general(a, b, _BATCH_NT, preferred_element_type=F32)


def _bdot(a, b):
    return lax.dot_general(a, b, _BATCH_NN, preferred_element_type=F32)


def _page_pair(a_ref, b_ref):
    return jnp.concatenate([a_ref[...], b_ref[...]], axis=2).astype(BF16)


def _new_slabs(kn_ref, vn_ref):
    pad = jnp.zeros((N_HEADS, LANES - QP, HEAD_DIM), F32)
    return (jnp.concatenate([kn_ref[...], pad], axis=1).astype(BF16),
            jnp.concatenate([vn_ref[...], pad], axis=1).astype(BF16))


def _split_pages(refs):
    n = 2 * BLOCKS_PER_STEP
    pair = lambda rs: [rs[2 * u:2 * u + 2] for u in range(BLOCKS_PER_STEP)]
    return pair(refs[:n]), pair(refs[n:2 * n]), refs[2 * n:]


def _moba_decode_kernel(pt_ref, q_ref, kn_ref, vn_ref, gt_ref, *refs, nblk):
    kp, vp, (bt_ref, bo_ref, o_ref, m_sc, l_sc, g_sc, acc_sc) = _split_pages(refs)
    t = pl.program_id(1)
    qb = (q_ref[...] * HEAD_DIM ** -0.5).astype(BF16)
    raws = [_bdot(qb, _page_pair(*kp[u])) for u in range(BLOCKS_PER_STEP)]
    for u, raw in enumerate(raws):
        n = t * BLOCKS_PER_STEP + u
        s = raw + bt_ref[u]
        m = jnp.max(s, axis=2, keepdims=True)
        p = jnp.exp(s - m)
        m_sc[n] = m
        l_sc[n] = jnp.sum(p, axis=2, keepdims=True)
        acc_sc[n] = _bdot_nt(p.astype(BF16), _page_pair(*vp[u]))
        g_sc[n] = jnp.sum(raw, axis=2, keepdims=True) * (1.0 / MOBA_BLOCK)

    @pl.when(t == nblk // BLOCKS_PER_STEP - 1)
    def _():
        g = g_sc[...]
        n_iota = lax.broadcasted_iota(jnp.int32, g.shape, 0)
        sel = jnp.zeros(g.shape, jnp.bool_)
        for _ in range(MOBA_TOPK):
            mx = jnp.max(g, axis=0, keepdims=True)
            cand = (g == mx) & jnp.logical_not(sel)
            idx = jnp.min(jnp.where(cand, n_iota, nblk), axis=0, keepdims=True)
            pick = n_iota == idx
            sel = sel | pick
            g = jnp.where(pick, -jnp.inf, g)
        kn, vn = _new_slabs(kn_ref, vn_ref)
        so = _bdot_nt(qb, kn) + bo_ref[...]
        mo = jnp.max(so, axis=2, keepdims=True)
        po = jnp.exp(so - mo)
        lo = jnp.sum(po, axis=2, keepdims=True)
        acc_o = _bdot(po.astype(BF16), vn)
        mb = m_sc[...]
        mt = jnp.maximum(mo, jnp.max(jnp.where(sel, mb, NEG), axis=0))
        w = jnp.where(sel, jnp.exp(mb - mt), 0.0)
        wo = jnp.exp(mo - mt)
        lt = wo * lo + jnp.sum(w * l_sc[...], axis=0)
        o = wo * acc_o + jnp.sum(w * acc_sc[...], axis=0)
        o_ref[...] = (o / lt) * _silu(gt_ref[...])


def _stick_decode_kernel(pt_ref, q_ref, kn_ref, vn_ref, gt_ref, *refs, nblk):
    kp, vp, (uo_ref, ub_ref, o_ref, c_sc, acc_sc) = _split_pages(refs)
    t = pl.program_id(1)
    qb = (q_ref[...] * HEAD_DIM ** -0.5).astype(BF16)

    def log_weights(z, u_ref, past):
        sp = jnp.maximum(z, 0.0) + jnp.log(1.0 + jnp.exp(-jnp.abs(z)))
        ln = -sp
        if past is not None:
            ln = jnp.where(past, ln, 0.0)
        ln2 = ln.reshape(N_HEADS * QP, ln.shape[2])
        hi = ln2.astype(BF16)
        lo = (ln2 - hi.astype(F32)).astype(BF16)
        suf = _dot(jnp.concatenate([hi, lo], axis=1), u_ref[...]).reshape(ln.shape)
        return (z - sp) + suf, jnp.sum(ln, axis=2, keepdims=True)

    @pl.when(t == 0)
    def _():
        kn, vn = _new_slabs(kn_ref, vn_ref)
        shape = (N_HEADS, QP, LANES)
        past = lax.broadcasted_iota(jnp.int32, shape, 2) < lax.broadcasted_iota(jnp.int32, shape, 1)
        la, dc = log_weights(_bdot_nt(qb, kn), uo_ref, past)
        acc_sc[...] = _bdot(jnp.where(past, jnp.exp(la), 0.0).astype(BF16), vn)
        c_sc[...] = dc

    zs = [_bdot(qb, _page_pair(*kp[u])) for u in range(BLOCKS_PER_STEP)]
    parts = [log_weights(z, ub_ref, None) for z in zs]
    c, acc = c_sc[...], acc_sc[...]
    for u, (la, dc) in enumerate(parts):
        acc = acc + _bdot_nt(jnp.exp(la + c).astype(BF16), _page_pair(*vp[u]))
        c = c + dc
    c_sc[...] = c
    acc_sc[...] = acc

    @pl.when(t == nblk // BLOCKS_PER_STEP - 1)
    def _():
        o_ref[...] = acc_sc[...] * _silu(gt_ref[...])


def _diff_decode_kernel(pt_ref, lam_ref, q_ref, kn_ref, vn_ref, gt_ref, *refs, nblk, lambda_init):
    kp, vp, (bt_ref, bo_ref, sg_ref, o_ref, m_sc, l_sc, acc_sc) = _split_pages(refs)
    t = pl.program_id(1)
    q = q_ref[...] * DIFF_QK_DIM ** -0.5
    lane = lax.broadcasted_iota(jnp.int32, q.shape, 2)
    first = lane < DIFF_QK_DIM
    qb = jnp.concatenate([jnp.where(first, q, 0.0), jnp.where(first, 0.0, q)],
                         axis=1).astype(BF16)
    both = lambda x: jnp.concatenate([x, x], axis=1)

    @pl.when(t == 0)
    def _():
        kn, vn = _new_slabs(kn_ref, vn_ref)
        so = _bdot_nt(qb, kn) + both(bo_ref[...])
        mo = jnp.max(so, axis=2, keepdims=True)
        po = jnp.exp(so - mo)
        m_sc[...] = mo
        l_sc[...] = jnp.sum(po, axis=2, keepdims=True)
        acc_sc[...] = _bdot(po.astype(BF16), vn)

    along_keys = lambda xs: jnp.concatenate(xs, axis=2)
    s = _bdot(qb, along_keys([_page_pair(*kp[u]) for u in range(BLOCKS_PER_STEP)]))
    s = s + along_keys([both(bt_ref[u]) for u in range(BLOCKS_PER_STEP)])
    m_old = m_sc[...]
    m = jnp.maximum(m_old, jnp.max(s, axis=2, keepdims=True))
    alpha = jnp.exp(m_old - m)
    p = jnp.exp(s - m)
    m_sc[...] = m
    l_sc[...] = alpha * l_sc[...] + jnp.sum(p, axis=2, keepdims=True)
    acc_sc[...] = alpha * acc_sc[...] + _bdot_nt(
        p.astype(BF16), along_keys([_page_pair(*vp[u]) for u in range(BLOCKS_PER_STEP)]))

    @pl.when(t == nblk // BLOCKS_PER_STEP - 1)
    def _():
        o = acc_sc[...] / l_sc[...]
        o = o[:, :QP] - lam_ref[0] * o[:, QP:]
        ms = jnp.mean(o * o, axis=2, keepdims=True)
        o = (o * lax.rsqrt(ms + SUBLN_EPS)) * sg_ref[...] * (1.0 - lambda_init)
        o_ref[...] = o * _silu(gt_ref[...])


def _decode_bias_tables(bias_table, past, ds, nblk):
    qi = jnp.arange(ds)
    kk = jnp.arange(MOBA_BLOCK)
    dist = past + qi[None, :, None] - (jnp.arange(nblk)[:, None, None] * MOBA_BLOCK + kk[None, None, :])
    bt = jnp.transpose(_rel_bias(bias_table, dist), (1, 0, 2, 3))
    ki = jnp.arange(LANES)
    ok = (ki[None, :] <= qi[:, None]) & (ki[None, :] < ds)
    bo = jnp.where(ok[None], _rel_bias(bias_table, qi[:, None] - ki[None, :]), NEG)
    padq = lambda x: jnp.pad(x, [(0, 0)] * (x.ndim - 2) + [(0, QP - ds), (0, 0)])
    return padq(bt).astype(F32), padq(bo).astype(F32)


def _decode_attention(kind, layer, q, k, v, gate, cache_k, cache_v, page_table, bias_table, extra):
    DB, n_pages = page_table.shape
    page = cache_k.shape[2]
    ds = q.shape[0] // DB
    assert MOBA_BLOCK == 2 * page and n_pages % (2 * BLOCKS_PER_STEP) == 0 and ds <= QP
    nblk = n_pages // 2
    past = n_pages * page
    pt = page_table.reshape(-1).astype(jnp.int32)

    def head_major(x):
        x = jnp.transpose(x.reshape(DB, ds, N_HEADS, HEAD_DIM), (0, 2, 1, 3))
        return jnp.pad(x, ((0, 0), (0, 0), (0, QP - ds), (0, 0)))

    reverse = kind == 1

    def page_spec(u, which):
        def index(b, t, pt_ref, *_):
            blk = t * BLOCKS_PER_STEP + u
            if reverse:
                blk = nblk - 1 - blk
            return (layer, pt_ref[b * n_pages + blk * 2 + which], 0, 0, 0)
        return pl.BlockSpec((None, None, N_HEADS, HEAD_DIM, page), index)

    keys_minor = lambda c: jnp.transpose(c, (0, 1, 3, 4, 2))
    cache_k, cache_v = keys_minor(cache_k), keys_minor(cache_v)

    tok_spec = pl.BlockSpec((None, N_HEADS, QP, HEAD_DIM), lambda b, t, *_: (b, 0, 0, 0))
    fixed = lambda shape: pl.BlockSpec(shape, lambda b, t, *_: (0,) * len(shape))
    blk_bias = pl.BlockSpec((BLOCKS_PER_STEP, N_HEADS, QP, MOBA_BLOCK),
                            lambda b, t, *_: (t, 0, 0, 0))
    step_pages = [page_spec(u, w) for u in range(BLOCKS_PER_STEP) for w in range(2)]
    pages = [cache_k] * len(step_pages) + [cache_v] * len(step_pages)
    page_specs = step_pages + step_pages
    toks = [head_major(q), head_major(k), head_major(v), head_major(gate)]
    tok_specs = [tok_spec] * 4
    vm = lambda *shape: pltpu.VMEM(shape, F32)
    if kind == 0:
        bt, bo = _decode_bias_tables(bias_table, past, ds, nblk)
        body = functools.partial(_moba_decode_kernel, nblk=nblk)
        args = [pt, *toks, *pages, bt, bo]
        in_specs = [*tok_specs, *page_specs, blk_bias, fixed((N_HEADS, QP, LANES))]
        scratch = [vm(nblk, N_HEADS, QP, 1), vm(nblk, N_HEADS, QP, 1), vm(nblk, N_HEADS, QP, 1),
                   vm(nblk, N_HEADS, QP, HEAD_DIM)]
        num_prefetch = 1
    elif kind == 1:
        def umat(n):
            r = jnp.arange(2 * n)[:, None] % n
            return (r > jnp.arange(n)[None, :]).astype(BF16)
        body = functools.partial(_stick_decode_kernel, nblk=nblk)
        args = [pt, *toks, *pages, umat(LANES), umat(MOBA_BLOCK)]
        in_specs = [*tok_specs, *page_specs, fixed((2 * LANES, LANES)),
                    fixed((2 * MOBA_BLOCK, MOBA_BLOCK))]
        scratch = [vm(N_HEADS, QP, 1), vm(N_HEADS, QP, HEAD_DIM)]
        num_prefetch = 1
    else:
        lam, lambda_init, subln_g = extra
        bt, bo = _decode_bias_tables(bias_table, past, ds, nblk)
        body = functools.partial(_diff_decode_kernel, nblk=nblk, lambda_init=lambda_init)
        args = [pt, lam.reshape(1), *toks, *pages, bt, bo, subln_g.reshape(1, 1, HEAD_DIM)]
        in_specs = [*tok_specs, *page_specs, blk_bias, fixed((N_HEADS, QP, LANES)),
                    fixed((1, 1, HEAD_DIM))]
        scratch = [vm(N_HEADS, 2 * QP, 1), vm(N_HEADS, 2 * QP, 1), vm(N_HEADS, 2 * QP, HEAD_DIM)]
        num_prefetch = 2
    o4 = pl.pallas_call(
        body,
        grid_spec=pltpu.PrefetchScalarGridSpec(
            num_scalar_prefetch=num_prefetch,
            grid=(DB, nblk // BLOCKS_PER_STEP),
            in_specs=in_specs,
            out_specs=tok_spec,
            scratch_shapes=scratch),
        out_shape=jax.ShapeDtypeStruct((DB, N_HEADS, QP, HEAD_DIM), F32),
        compiler_params=pltpu.CompilerParams(
            dimension_semantics=("arbitrary", "arbitrary"), vmem_limit_bytes=VMEM_LIMIT),
        name=("moba", "stick", "diff")[kind] + "_decode",
    )(*args)
    return jnp.transpose(o4[:, :, :ds], (0, 2, 1, 3)).reshape(DB * ds, D_MODEL)


def kernel(x_prompt, x_sample, cache_k, cache_v, page_table, bias_table, norm_g, w_in, w_out,
           diff_lambda_q1, diff_lambda_k1, diff_lambda_q2, diff_lambda_k2, diff_subln_g,
           final_norm_g):
    lams = (diff_lambda_q1, diff_lambda_k1, diff_lambda_q2, diff_lambda_k2)
    y_p, k_p, v_p = _prompt_trunk(x_prompt, bias_table, norm_g, w_in, w_out, *lams,
                                  diff_subln_g, final_norm_g)
    y_s, k_s, v_s = _sample_trunk(x_sample, cache_k, cache_v, page_table, bias_table, norm_g,
                                  w_in, w_out, *lams, diff_subln_g, final_norm_g)
    return (y_p, y_s, k_p, v_p, k_s, v_s)
```

```python
import functools
import math

import jax
import jax.numpy as jnp
from jax import lax
from jax.experimental import pallas as pl
from jax.experimental.pallas import tpu as pltpu

N_HEADS = 16
HEAD_DIM = 64
D_MODEL = N_HEADS * HEAD_DIM
DIFF_QK_DIM = HEAD_DIM // 2
N_MIXERS = 3
MOBA_BLOCK = 256
MOBA_TOPK = 3
N_BUCKETS = 32
MAX_DISTANCE = 128
RMS_EPS = 1e-6
SUBLN_EPS = 1e-5

LANES = 128
HEADS_PER_TILE = LANES // HEAD_DIM
N_PAIRS = N_HEADS // HEADS_PER_TILE
TQ = MOBA_BLOCK
NEG = -1e30
LOG2E = math.log2(math.e)
EXIT_LOG2 = -160.0
VMEM_LIMIT = 48 * 1024 * 1024
F32 = jnp.float32
BF16 = jnp.bfloat16

def _dot(a, b):
    return jnp.dot(a, b, preferred_element_type=F32)


def _silu(x):
    return x / (1.0 + jnp.exp(-x))


def _proj_kernel(*refs, has_out, has_in, tm, kv_seq_minor):
    it = iter(refs)
    x_ref = next(it)
    og_ref = next(it) if has_out else None
    wo_ref = next(it) if has_out else None
    g_ref = next(it)
    wi_ref = next(it) if has_in else None
    if has_out and has_in:
        xn_ref = next(it)
    if has_in:
        q_ref, k_ref, v_ref, gt_ref, kb_ref, vt_ref, km_ref = (next(it) for _ in range(7))
    else:
        y_ref = next(it)

    x = x_ref[...]
    if has_out:
        x = x + _dot(og_ref[...].astype(BF16), wo_ref[...])
        if has_in:
            xn_ref[...] = x
    r = lax.rsqrt(jnp.mean(x * x, axis=-1, keepdims=True) + RMS_EPS)
    h = (x * r) * g_ref[...]
    if not has_in:
        y_ref[...] = h
        return
    acc = _dot(h.astype(BF16), wi_ref[...])
    q_ref[...] = acc[:, 0 * D_MODEL:1 * D_MODEL]
    k = acc[:, 1 * D_MODEL:2 * D_MODEL]
    v = acc[:, 2 * D_MODEL:3 * D_MODEL]
    k_ref[...] = k.T if kv_seq_minor else k
    v_ref[...] = v.T if kv_seq_minor else v
    gt_ref[...] = acc[:, 3 * D_MODEL:4 * D_MODEL]
    kb_ref[...] = k.astype(BF16)
    nblk = tm // MOBA_BLOCK
    if nblk:
        km_ref[...] = jnp.sum(k.reshape(nblk, MOBA_BLOCK, D_MODEL), axis=1) * (1.0 / MOBA_BLOCK)
        row = lax.broadcasted_iota(jnp.int32, (N_PAIRS, LANES, MOBA_BLOCK), 1)
        for jb in range(nblk):
            vt = v[jb * MOBA_BLOCK:(jb + 1) * MOBA_BLOCK, :].T.reshape(N_PAIRS, LANES, MOBA_BLOCK)
            vt_ref[jb] = jnp.concatenate(
                [jnp.where(row < HEAD_DIM, vt, 0.0), jnp.where(row >= HEAD_DIM, vt, 0.0)],
                axis=2).astype(BF16)
    else:
        km_ref[...] = jnp.zeros(km_ref.shape, F32)
        vt_ref[...] = jnp.zeros(vt_ref.shape, BF16)


def _proj(x, og, w_out, g, w_in, seq=None):
    T = x.shape[0]
    tm = min(512, T)
    assert T % tm == 0 and (seq is None or seq % tm == 0)
    has_out, has_in = og is not None, w_in is not None
    row = lambda i: (i, 0)
    fixed = lambda i: (0, 0)
    args, in_specs = [x], [pl.BlockSpec((tm, D_MODEL), row)]
    if has_out:
        args += [og, w_out]
        in_specs += [pl.BlockSpec((tm, D_MODEL), row), pl.BlockSpec((D_MODEL, D_MODEL), fixed)]
    args.append(g.reshape(1, D_MODEL))
    in_specs.append(pl.BlockSpec((1, D_MODEL), fixed))
    out_shape, out_specs = [], []
    if has_in:
        args.append(w_in)
        in_specs.append(pl.BlockSpec((D_MODEL, 4 * D_MODEL), fixed))
        if has_out:
            out_shape.append(jax.ShapeDtypeStruct((T, D_MODEL), F32))
            out_specs.append(pl.BlockSpec((tm, D_MODEL), row))
        for name, dt in (("q", F32), ("k", F32), ("v", F32), ("gate", F32), ("kb", BF16)):
            if seq is not None and name in ("k", "v"):
                tiles = seq // tm
                out_shape.append(jax.ShapeDtypeStruct((T // seq, D_MODEL, seq), dt))
                out_specs.append(pl.BlockSpec((None, D_MODEL, tm),
                                              lambda i: (i // tiles, 0, i % tiles)))
            else:
                out_shape.append(jax.ShapeDtypeStruct((T, D_MODEL), dt))
                out_specs.append(pl.BlockSpec((tm, D_MODEL), row))
        nblk = max(tm // MOBA_BLOCK, 1)
        out_shape.append(jax.ShapeDtypeStruct(
            (T // tm * nblk, N_PAIRS, LANES, 2 * MOBA_BLOCK), BF16))
        out_specs.append(pl.BlockSpec((nblk, N_PAIRS, LANES, 2 * MOBA_BLOCK),
                                      lambda i: (i, 0, 0, 0)))
        out_shape.append(jax.ShapeDtypeStruct((T // tm, nblk, D_MODEL), F32))
        out_specs.append(pl.BlockSpec((None, nblk, D_MODEL), lambda i: (i, 0, 0)))
    else:
        out_shape.append(jax.ShapeDtypeStruct((T, D_MODEL), F32))
        out_specs.append(pl.BlockSpec((tm, D_MODEL), row))
    outs = pl.pallas_call(
        functools.partial(_proj_kernel, has_out=has_out, has_in=has_in, tm=tm,
                          kv_seq_minor=seq is not None),
        grid=(T // tm,),
        in_specs=in_specs,
        out_specs=out_specs,
        out_shape=out_shape,
        compiler_params=pltpu.CompilerParams(
            dimension_semantics=("arbitrary",), vmem_limit_bytes=VMEM_LIMIT),
        name=f"proj_o{int(has_out)}_i{int(has_in)}",
    )(*args)
    return outs


def _t5_bucket(n):
    n = jnp.maximum(n, 0)
    max_exact = N_BUCKETS // 2
    nf = jnp.maximum(n, 1).astype(F32)
    large = max_exact + (jnp.log(nf / max_exact) / math.log(MAX_DISTANCE / max_exact)
                         * (N_BUCKETS - max_exact)).astype(jnp.int32)
    large = jnp.minimum(large, N_BUCKETS - 1)
    return jnp.where(n < max_exact, n, large)


def _rel_bias(bias_table, dist):
    onehot = (_t5_bucket(dist)[..., None] == jnp.arange(N_BUCKETS)).astype(F32)
    shifted = bias_table - bias_table[N_BUCKETS - 1]
    b = jnp.einsum('...n,nh->...h', onehot, shifted, precision=lax.Precision.HIGHEST)
    return jnp.moveaxis(b, -1, 0)


def _prompt_bias_tiles(bias_table):
    ki = jnp.arange(TQ)[:, None]
    qi = jnp.arange(TQ)[None, :]
    diag = jnp.where(ki <= qi, _rel_bias(bias_table, qi - ki) * LOG2E, NEG)
    adj = _rel_bias(bias_table, TQ + qi - ki) * LOG2E
    return diag.astype(F32), adj.astype(F32)


def _head_lane_masks(shape):
    lane = lax.broadcasted_iota(jnp.int32, shape, len(shape) - 1)
    return [(lane >= h * HEAD_DIM) & (lane < (h + 1) * HEAD_DIM) for h in range(HEADS_PER_TILE)]


def _row_masks():
    row = lax.broadcasted_iota(jnp.int32, (LANES, TQ), 0)
    return [(row >= h * HEAD_DIM) & (row < (h + 1) * HEAD_DIM) for h in range(HEADS_PER_TILE)]


def _pair_rows(x0, x1):
    return jnp.concatenate([jnp.broadcast_to(x0, (HEAD_DIM, TQ)),
                            jnp.broadcast_to(x1, (HEAD_DIM, TQ))], axis=0)


def _softmax_step_t(s, m, l):
    m_new = jnp.maximum(m, jnp.max(s, axis=0, keepdims=True))
    alpha = jnp.exp2(m - m_new)
    p = jnp.exp2(s - m_new)
    l = alpha * l + jnp.sum(p, axis=0, keepdims=True)
    return m_new, l, alpha, p.astype(BF16)


def _key_block(k_ref, j):
    return k_ref[pl.ds(pl.multiple_of(j * TQ, TQ), TQ), :]


PAIRS_PER_TRIP = 4


def _run_pipelined(scores, update, buf_a, buf_b, carry, n):
    last = jnp.maximum(n - 1, 0)

    def pair(t, c):
        j = 2 * t
        scores(j + 1, buf_b)
        c = update(buf_a, j, c)
        scores(jnp.minimum(j + 2, last), buf_a)
        return update(buf_b, j + 1, c)

    def pairs(t, c):
        for u in range(PAIRS_PER_TRIP):
            c = pair(t * PAIRS_PER_TRIP + u, c)
        return c

    n_pairs = n // 2
    n_trips = n_pairs // PAIRS_PER_TRIP
    carry = lax.fori_loop(0, n_trips, pairs, carry)
    carry = lax.fori_loop(n_trips * PAIRS_PER_TRIP, n_pairs, pair, carry)
    return lax.cond(n % 2 == 1, lambda c: update(buf_a, last, c), lambda c: c, carry)


def _moba_kernel(q_ref, k_ref, vt_ref, km_ref, gt_ref, td_ref, ta_ref, o_ref, sa_ref, sb_ref, *,
                 nbp):
    i = pl.program_id(2)
    qT = q_ref[...].T
    rmask = _row_masks()
    n_iota = lax.broadcasted_iota(jnp.int32, (nbp, TQ), 0)
    valid = n_iota < i
    lane = lax.broadcasted_iota(jnp.int32, (TQ, LANES), 1)
    zeros_b = jnp.zeros((TQ, LANES), BF16)

    q_aug = []
    for h in range(HEADS_PER_TILE):
        qm = jnp.where(rmask[h], qT, 0.0)
        g = jnp.dot(km_ref[...], qm, preferred_element_type=F32,
                    precision=lax.Precision.HIGHEST)
        g = jnp.where(valid, g, -jnp.inf)
        sel = jnp.zeros(g.shape, jnp.bool_)
        for _ in range(MOBA_TOPK):
            mx = jnp.max(g, axis=0, keepdims=True)
            cand = (g == mx) & valid & jnp.logical_not(sel)
            idx = jnp.min(jnp.where(cand, n_iota, nbp), axis=0, keepdims=True)
            pick = n_iota == idx
            sel = sel | pick
            g = jnp.where(pick, -jnp.inf, g)
        selb = jnp.where(sel, 0.0, NEG)
        q_aug.append(jnp.concatenate(
            [qm * (HEAD_DIM ** -0.5 * LOG2E), selb, jnp.zeros((LANES - nbp, TQ), F32)],
            axis=0).astype(BF16))

    def scores(j, buf, onehot=True):
        ej = jnp.where(lane == j, 1.0, 0.0).astype(BF16) if onehot else zeros_b
        k_aug = jnp.concatenate([_key_block(k_ref, j), ej], axis=1)
        for h in range(HEADS_PER_TILE):
            buf[h] = _dot(k_aug, q_aug[h])

    def update(buf, j, carry, bias=None):
        stats, acc = carry
        new, alphas, ps = [], [], []
        for h in range(HEADS_PER_TILE):
            s = buf[h] if bias is None else buf[h] + bias[h]
            m, l, alpha, p = _softmax_step_t(s, *stats[h])
            new.append((m, l))
            alphas.append(alpha)
            ps.append(p)
        acc = acc * _pair_rows(*alphas) + _dot(vt_ref[j], jnp.concatenate(ps, axis=0))
        return tuple(new), acc

    init = (tuple((jnp.full((1, TQ), NEG, F32), jnp.zeros((1, TQ), F32))
                  for _ in range(HEADS_PER_TILE)), jnp.zeros((LANES, TQ), F32))
    prev = jnp.maximum(i - 1, 0)
    scores(i, sa_ref, onehot=False)
    scores(prev, sb_ref)
    carry = update(sa_ref, i, init, td_ref)
    scores(0, sa_ref)
    carry = update(sb_ref, prev, carry, ta_ref)
    stats, acc = _run_pipelined(scores, update, sa_ref, sb_ref, carry, jnp.maximum(i - 1, 0))
    o = (acc * _pair_rows(1.0 / stats[0][1], 1.0 / stats[1][1])).T
    o_ref[...] = (o * _silu(gt_ref[...])).astype(BF16)


def _stick_kernel(q_ref, k_ref, vt_ref, gt_ref, u_ref, o_ref):
    i = pl.program_id(2)
    qT = q_ref[...].T
    rmask = _row_masks()
    qs = [(jnp.where(rmask[h], qT, 0.0) * (HEAD_DIM ** -0.5 * LOG2E)).astype(BF16)
          for h in range(HEADS_PER_TILE)]
    ki = lax.broadcasted_iota(jnp.int32, (TQ, TQ), 0)
    qi = lax.broadcasted_iota(jnp.int32, (TQ, TQ), 1)
    past = ki < qi

    def step(j, carry, diag):
        cs, acc = carry
        kj = _key_block(k_ref, j)
        new_c, ws = [], []
        for h in range(HEADS_PER_TILE):
            z = _dot(kj, qs[h])
            sp = jnp.maximum(z, 0.0) + jnp.log2(1.0 + jnp.exp2(-jnp.abs(z)))
            ln = -sp
            if diag:
                ln = jnp.where(past, ln, 0.0)
            hi = ln.astype(BF16)
            lo = (ln - hi.astype(F32)).astype(BF16)
            suf = _dot(u_ref[...], jnp.concatenate([hi, lo], axis=0))
            a = jnp.exp2((z - sp) + suf + cs[h])
            if diag:
                a = jnp.where(past, a, 0.0)
            ws.append(a.astype(BF16))
            new_c.append(cs[h] + jnp.sum(ln, axis=0, keepdims=True))
        acc = acc + _dot(vt_ref[j], jnp.concatenate(ws, axis=0))
        return tuple(new_c), acc

    def left(cs):
        return jnp.max(jnp.maximum(cs[0], cs[1]))

    init = (tuple(jnp.zeros((1, TQ), F32) for _ in range(HEADS_PER_TILE)),
            jnp.zeros((LANES, TQ), F32))
    carry = step(i, init, True)

    def cond(state):
        t, cmax, _ = state
        return (t < i) & (cmax > EXIT_LOG2)

    def body(state):
        t, _, c = state
        c = step(i - 1 - t, c, False)
        return t + 1, left(c[0]), c

    _, _, carry = lax.while_loop(cond, body, (jnp.int32(0), left(carry[0]), carry))
    o_ref[...] = (carry[1].T * _silu(gt_ref[...])).astype(BF16)


def _diff_finish(o, hmask, sg_ref, lambda_init):
    o2 = o * o
    r = jnp.zeros(o.shape, F32)
    for h in range(HEADS_PER_TILE):
        ms = jnp.sum(jnp.where(hmask[h], o2, 0.0), axis=1, keepdims=True) * (1.0 / HEAD_DIM)
        r = jnp.where(hmask[h], lax.rsqrt(ms + SUBLN_EPS), r)
    return (o * r) * sg_ref[...] * (1.0 - lambda_init)


def _diff_kernel(lam_ref, q_ref, k_ref, vt_ref, gt_ref, td_ref, ta_ref, sg_ref, o_ref,
                 sa_ref, sb_ref, *, lambda_init):
    i = pl.program_id(2)
    qT = q_ref[...].T
    row = lax.broadcasted_iota(jnp.int32, qT.shape, 0)
    qm = [[None, None] for _ in range(HEADS_PER_TILE)]
    for h in range(HEADS_PER_TILE):
        for c in range(2):
            lo = h * HEAD_DIM + c * DIFF_QK_DIM
            msk = (row >= lo) & (row < lo + DIFF_QK_DIM)
            qm[h][c] = (jnp.where(msk, qT, 0.0) * (DIFF_QK_DIM ** -0.5 * LOG2E)).astype(BF16)

    def scores(j, buf):
        kj = _key_block(k_ref, j)
        for c in range(2):
            for h in range(HEADS_PER_TILE):
                buf[c * HEADS_PER_TILE + h] = _dot(kj, qm[h][c])

    def update(buf, j, carry, bias=None):
        vt = vt_ref[j]
        out = []
        for c in range(2):
            stats, acc = carry[c]
            new, alphas, ps = [], [], []
            for h in range(HEADS_PER_TILE):
                s = buf[c * HEADS_PER_TILE + h]
                if bias is not None:
                    s = s + bias[h]
                m, l, alpha, p = _softmax_step_t(s, *stats[h])
                new.append((m, l))
                alphas.append(alpha)
                ps.append(p)
            acc = acc * _pair_rows(*alphas) + _dot(vt, jnp.concatenate(ps, axis=0))
            out.append((tuple(new), acc))
        return tuple(out)

    init = tuple((tuple((jnp.full((1, TQ), NEG, F32), jnp.zeros((1, TQ), F32))
                        for _ in range(HEADS_PER_TILE)), jnp.zeros((LANES, TQ), F32))
                 for _ in range(2))
    prev = jnp.maximum(i - 1, 0)
    scores(i, sa_ref)
    scores(prev, sb_ref)
    carry = update(sa_ref, i, init, td_ref)
    scores(0, sa_ref)
    adj = update(sb_ref, prev, carry, ta_ref)
    carry = jax.tree_util.tree_map(lambda n, o: jnp.where(i > 0, n, o), adj, carry)
    carry = _run_pipelined(scores, update, sa_ref, sb_ref, carry, jnp.maximum(i - 1, 0))
    outs =[acc * _pair_rows(1.0 / stats[0][1], 1.0 / stats[1][1]) for stats, acc in carry]
    o = (outs[0] - lam_ref[0] * outs[1]).T
    o = _diff_finish(o, _head_lane_masks(o.shape), sg_ref, lambda_init)
    o_ref[...] = (o * _silu(gt_ref[...])).astype(BF16)


def _prompt_attention(kind, q, kb, vt, gate, kmean, tiles, extra, B, S):
    nq = S // TQ
    q3, g3 = q.reshape(B, S, D_MODEL), gate.reshape(B, S, D_MODEL)
    k3 = kb.reshape(B, S, D_MODEL)
    v3 = vt.reshape(B, nq, N_PAIRS, LANES, 2 * TQ)
    tile_spec = pl.BlockSpec((None, TQ, LANES), lambda b, p, i, *_: (b, i, p))
    seq_spec = pl.BlockSpec((None, S, LANES), lambda b, p, i, *_: (b, 0, p))
    vt_spec = pl.BlockSpec((None, nq, None, LANES, 2 * TQ), lambda b, p, i, *_: (b, 0, p, 0, 0))
    bias_spec = pl.BlockSpec((HEADS_PER_TILE, TQ, TQ), lambda b, p, i, *_: (p, 0, 0))
    td, ta = tiles
    num_prefetch = 0
    n_tiles = {0: HEADS_PER_TILE, 1: 0, 2: 2 * HEADS_PER_TILE}[kind]
    scratch = [pltpu.VMEM((n_tiles, TQ, TQ), F32)] * 2 if n_tiles else []
    if kind == 0:
        nbp = -(-nq // 8) * 8
        km = jnp.pad(kmean.reshape(B, nq, D_MODEL), ((0, 0), (0, nbp - nq), (0, 0)))
        body = functools.partial(_moba_kernel, nbp=nbp)
        args = [q3, k3, v3, km, g3, td, ta]
        in_specs = [tile_spec, seq_spec, vt_spec,
                    pl.BlockSpec((None, nbp, LANES), lambda b, p, i: (b, 0, p)),
                    tile_spec, bias_spec, bias_spec]
    elif kind == 1:
        c = jnp.arange(2 * TQ)[None, :] % TQ
        u = (c > jnp.arange(TQ)[:, None]).astype(BF16)
        body = _stick_kernel
        args = [q3, k3, v3, g3, u]
        in_specs = [tile_spec, seq_spec, vt_spec, tile_spec,
                    pl.BlockSpec((TQ, 2 * TQ), lambda b, p, i: (0, 0))]
    else:
        lam, lambda_init, subln_g = extra
        sg = jnp.tile(subln_g, HEADS_PER_TILE).reshape(1, LANES)
        body = functools.partial(_diff_kernel, lambda_init=lambda_init)
        args = [lam.reshape(1), q3, k3, v3, g3, td, ta, sg]
        num_prefetch = 1
        in_specs = [tile_spec, seq_spec, vt_spec, tile_spec, bias_spec, bias_spec,
                    pl.BlockSpec((1, LANES), lambda b, p, i, *_: (0, 0))]
    og = pl.pallas_call(
        body,
        grid_spec=pltpu.PrefetchScalarGridSpec(
            num_scalar_prefetch=num_prefetch,
            grid=(B, N_PAIRS, nq),
            in_specs=in_specs,
            out_specs=tile_spec,
            scratch_shapes=scratch),
        out_shape=jax.ShapeDtypeStruct((B, S, D_MODEL), BF16),
        compiler_params=pltpu.CompilerParams(
            dimension_semantics=("arbitrary", "arbitrary", "arbitrary"),
            vmem_limit_bytes=VMEM_LIMIT),
        name=("moba", "stick", "diff")[kind] + "_prompt",
    )(*args)
    return og.reshape(B * S, D_MODEL)


def _diff_lambda(i, lq1, lk1, lq2, lk2):
    j = i // N_MIXERS
    lambda_init = 0.8 - 0.6 * math.exp(-0.3 * i)
    lam = (jnp.exp(jnp.sum(lq1[j] * lk1[j]).astype(F32))
           - jnp.exp(jnp.sum(lq2[j] * lk2[j]).astype(F32)) + lambda_init)
    return lam.astype(F32), lambda_init


def _prompt_trunk(x_prompt, bias_table, norm_g, w_in, w_out, lq1, lk1, lq2, lk2, subln_g, final_g):
    B, S, _ = x_prompt.shape
    depth = w_in.shape[0]
    assert S % TQ == 0 and S // TQ <= LANES
    tiles = _prompt_bias_tiles(bias_table)
    x = x_prompt.reshape(B * S, D_MODEL)
    og = None
    new_k, new_v = [], []
    for i in range(depth):
        outs = _proj(x, og, None if og is None else w_out[i - 1].astype(BF16),
                     norm_g[i], w_in[i].astype(BF16), seq=S)
        if og is not None:
            x, outs = outs[0], outs[1:]
        q, k, v, gate, kb, vt, kmean = outs
        new_k.append(k)
        new_v.append(v)
        kind = i % N_MIXERS
        extra = None
        if kind == 2:
            lam, lambda_init = _diff_lambda(i, lq1, lk1, lq2, lk2)
            extra = (lam, lambda_init, subln_g[i // N_MIXERS])
        og = _prompt_attention(kind, q, kb, vt, gate, kmean, tiles, extra, B, S)
    y = _proj(x, og, w_out[depth - 1].astype(BF16), final_g, None)[0]

    def rows(stacked):
        return jnp.transpose(stacked.reshape(depth, B, N_HEADS, HEAD_DIM, S), (0, 1, 4, 2, 3))

    return y.reshape(B, S, D_MODEL), rows(jnp.stack(new_k)), rows(jnp.stack(new_v))


def _sample_trunk(x_sample, cache_k, cache_v, page_table, bias_table, norm_g, w_in, w_out,
                  lq1, lk1, lq2, lk2, subln_g, final_g):
    DB, ds, _ = x_sample.shape
    depth = w_in.shape[0]
    x = x_sample.reshape(DB * ds, D_MODEL)
    og = None
    new_k, new_v = [], []
    for i in range(depth):
        outs = _proj(x, og, None if og is None else w_out[i - 1].astype(BF16),
                     norm_g[i], w_in[i].astype(BF16))
        if og is not None:
            x, outs = outs[0], outs[1:]
        q, k, v, gate = outs[:4]
        new_k.append(k.reshape(DB, ds, N_HEADS, HEAD_DIM))
        new_v.append(v.reshape(DB, ds, N_HEADS, HEAD_DIM))
        kind = i % N_MIXERS
        extra = None
        if kind == 2:
            lam, lambda_init = _diff_lambda(i, lq1, lk1, lq2, lk2)
            extra = (lam, lambda_init, subln_g[i // N_MIXERS])
        og = _decode_attention(kind, i, q, k, v, gate, cache_k, cache_v, page_table, bias_table,
                               extra)
    y = _proj(x, og, w_out[depth - 1].astype(BF16), final_g, None)[0]
    return y.reshape(DB, ds, D_MODEL), jnp.stack(new_k), jnp.stack(new_v)


QP = 8
BLOCKS_PER_STEP = 4

_BATCH_NT = (((2,), (2,)), ((0,), (0,)))
_BATCH_NN = (((2,), (1,)), ((0,), (0,)))


def _bdot_nt(a, b):
    return lax.dot_general(a, b, _BATCH_NT, preferred_element_type=F32)


def _bdot(a, b):
    return lax.dot_general(a, b, _BATCH_NN, preferred_element_type=F32)


def _page_pair(a_ref, b_ref):
    return jnp.concatenate([a_ref[...], b_ref[...]], axis=2).astype(BF16)


def _new_slabs(kn_ref, vn_ref):
    pad = jnp.zeros((N_HEADS, LANES - QP, HEAD_DIM), F32)
    return (jnp.concatenate([kn_ref[...], pad], axis=1).astype(BF16),
            jnp.concatenate([vn_ref[...], pad], axis=1).astype(BF16))


def _split_pages(refs):
    n = 2 * BLOCKS_PER_STEP
    pair = lambda rs: [rs[2 * u:2 * u + 2] for u in range(BLOCKS_PER_STEP)]
    return pair(refs[:n]), pair(refs[n:2 * n]), refs[2 * n:]


def _moba_decode_kernel(pt_ref, q_ref, kn_ref, vn_ref, gt_ref, *refs, nblk):
    kp, vp, (bt_ref, bo_ref, o_ref, m_sc, l_sc, g_sc, acc_sc) = _split_pages(refs)
    t = pl.program_id(1)
    qb = (q_ref[...] * HEAD_DIM ** -0.5).astype(BF16)
    raws = [_bdot(qb, _page_pair(*kp[u])) for u in range(BLOCKS_PER_STEP)]
    for u, raw in enumerate(raws):
        n = t * BLOCKS_PER_STEP + u
        s = raw + bt_ref[u]
        m = jnp.max(s, axis=2, keepdims=True)
        p = jnp.exp(s - m)
        m_sc[n] = m
        l_sc[n] = jnp.sum(p, axis=2, keepdims=True)
        acc_sc[n] = _bdot_nt(p.astype(BF16), _page_pair(*vp[u]))
        g_sc[n] = jnp.sum(raw, axis=2, keepdims=True) * (1.0 / MOBA_BLOCK)

    @pl.when(t == nblk // BLOCKS_PER_STEP - 1)
    def _():
        g = g_sc[...]
        n_iota = lax.broadcasted_iota(jnp.int32, g.shape, 0)
        sel = jnp.zeros(g.shape, jnp.bool_)
        for _ in range(MOBA_TOPK):
            mx = jnp.max(g, axis=0, keepdims=True)
            cand = (g == mx) & jnp.logical_not(sel)
            idx = jnp.min(jnp.where(cand, n_iota, nblk), axis=0, keepdims=True)
            pick = n_iota == idx
            sel = sel | pick
            g = jnp.where(pick, -jnp.inf, g)
        kn, vn = _new_slabs(kn_ref, vn_ref)
        so = _bdot_nt(qb, kn) + bo_ref[...]
        mo = jnp.max(so, axis=2, keepdims=True)
        po = jnp.exp(so - mo)
        lo = jnp.sum(po, axis=2, keepdims=True)
        acc_o = _bdot(po.astype(BF16), vn)
        mb = m_sc[...]
        mt = jnp.maximum(mo, jnp.max(jnp.where(sel, mb, NEG), axis=0))
        w = jnp.where(sel, jnp.exp(mb - mt), 0.0)
        wo = jnp.exp(mo - mt)
        lt = wo * lo + jnp.sum(w * l_sc[...], axis=0)
        o = wo * acc_o + jnp.sum(w * acc_sc[...], axis=0)
        o_ref[...] = (o / lt) * _silu(gt_ref[...])


def _stick_decode_kernel(pt_ref, q_ref, kn_ref, vn_ref, gt_ref, *refs, nblk):
    kp, vp, (uo_ref, ub_ref, o_ref, c_sc, acc_sc) = _split_pages(refs)
    t = pl.program_id(1)
    qb = (q_ref[...] * HEAD_DIM ** -0.5).astype(BF16)

    def log_weights(z, u_ref, past):
        sp = jnp.maximum(z, 0.0) + jnp.log(1.0 + jnp.exp(-jnp.abs(z)))
        ln = -sp
        if past is not None:
            ln = jnp.where(past, ln, 0.0)
        ln2 = ln.reshape(N_HEADS * QP, ln.shape[2])
        hi = ln2.astype(BF16)
        lo = (ln2 - hi.astype(F32)).astype(BF16)
        suf = _dot(jnp.concatenate([hi, lo], axis=1), u_ref[...]).reshape(ln.shape)
        return (z - sp) + suf, jnp.sum(ln, axis=2, keepdims=True)

    @pl.when(t == 0)
    def _():
        kn, vn = _new_slabs(kn_ref, vn_ref)
        shape = (N_HEADS, QP, LANES)
        past = lax.broadcasted_iota(jnp.int32, shape, 2) < lax.broadcasted_iota(jnp.int32, shape, 1)
        la, dc = log_weights(_bdot_nt(qb, kn), uo_ref, past)
        acc_sc[...] = _bdot(jnp.where(past, jnp.exp(la), 0.0).astype(BF16), vn)
        c_sc[...] = dc

    zs = [_bdot(qb, _page_pair(*kp[u])) for u in range(BLOCKS_PER_STEP)]
    parts = [log_weights(z, ub_ref, None) for z in zs]
    c, acc = c_sc[...], acc_sc[...]
    for u, (la, dc) in enumerate(parts):
        acc = acc + _bdot_nt(jnp.exp(la + c).astype(BF16), _page_pair(*vp[u]))
        c = c + dc
    c_sc[...] = c
    acc_sc[...] = acc

    @pl.when(t == nblk // BLOCKS_PER_STEP - 1)
    def _():
        o_ref[...] = acc_sc[...] * _silu(gt_ref[...])


def _diff_decode_kernel(pt_ref, lam_ref, q_ref, kn_ref, vn_ref, gt_ref, *refs, nblk, lambda_init):
    kp, vp, (bt_ref, bo_ref, sg_ref, o_ref, m_sc, l_sc, acc_sc) = _split_pages(refs)
    t = pl.program_id(1)
    q = q_ref[...] * DIFF_QK_DIM ** -0.5
    lane = lax.broadcasted_iota(jnp.int32, q.shape, 2)
    first = lane < DIFF_QK_DIM
    qb = jnp.concatenate([jnp.where(first, q, 0.0), jnp.where(first, 0.0, q)],
                         axis=1).astype(BF16)
    both = lambda x: jnp.concatenate([x, x], axis=1)

    @pl.when(t == 0)
    def _():
        kn, vn = _new_slabs(kn_ref, vn_ref)
        so = _bdot_nt(qb, kn) + both(bo_ref[...])
        mo = jnp.max(so, axis=2, keepdims=True)
        po = jnp.exp(so - mo)
        m_sc[...] = mo
        l_sc[...] = jnp.sum(po, axis=2, keepdims=True)
        acc_sc[...] = _bdot(po.astype(BF16), vn)

    along_keys = lambda xs: jnp.concatenate(xs, axis=2)
    s = _bdot(qb, along_keys([_page_pair(*kp[u]) for u in range(BLOCKS_PER_STEP)]))
    s = s + along_keys([both(bt_ref[u]) for u in range(BLOCKS_PER_STEP)])
    m_old = m_sc[...]
    m = jnp.maximum(m_old, jnp.max(s, axis=2, keepdims=True))
    alpha = jnp.exp(m_old - m)
    p = jnp.exp(s - m)
    m_sc[...] = m
    l_sc[...] = alpha * l_sc[...] + jnp.sum(p, axis=2, keepdims=True)
    acc_sc[...] = alpha * acc_sc[...] + _bdot_nt(
        p.astype(BF16), along_keys([_page_pair(*vp[u]) for u in range(BLOCKS_PER_STEP)]))

    @pl.when(t == nblk // BLOCKS_PER_STEP - 1)
    def _():
        o = acc_sc[...] / l_sc[...]
        o = o[:, :QP] - lam_ref[0] * o[:, QP:]
        ms = jnp.mean(o * o, axis=2, keepdims=True)
        o = (o * lax.rsqrt(ms + SUBLN_EPS)) * sg_ref[...] * (1.0 - lambda_init)
        o_ref[...] = o * _silu(gt_ref[...])


def _decode_bias_tables(bias_table, past, ds, nblk):
    qi = jnp.arange(ds)
    kk = jnp.arange(MOBA_BLOCK)
    dist = past + qi[None, :, None] - (jnp.arange(nblk)[:, None, None] * MOBA_BLOCK + kk[None, None, :])
    bt = jnp.transpose(_rel_bias(bias_table, dist), (1, 0, 2, 3))
    ki = jnp.arange(LANES)
    ok = (ki[None, :] <= qi[:, None]) & (ki[None, :] < ds)
    bo = jnp.where(ok[None], _rel_bias(bias_table, qi[:, None] - ki[None, :]), NEG)
    padq = lambda x: jnp.pad(x, [(0, 0)] * (x.ndim - 2) + [(0, QP - ds), (0, 0)])
    return padq(bt).astype(F32), padq(bo).astype(F32)


def _decode_attention(kind, layer, q, k, v, gate, cache_k, cache_v, page_table, bias_table, extra):
    DB, n_pages = page_table.shape
    page = cache_k.shape[2]
    ds = q.shape[0] // DB
    assert MOBA_BLOCK == 2 * page and n_pages % (2 * BLOCKS_PER_STEP) == 0 and ds <= QP
    nblk = n_pages // 2
    past = n_pages * page
    pt = page_table.reshape(-1).astype(jnp.int32)

    def head_major(x):
        x = jnp.transpose(x.reshape(DB, ds, N_HEADS, HEAD_DIM), (0, 2, 1, 3))
        return jnp.pad(x, ((0, 0), (0, 0), (0, QP - ds), (0, 0)))

    reverse = kind == 1

    def page_spec(u, which):
        def index(b, t, pt_ref, *_):
            blk = t * BLOCKS_PER_STEP + u
            if reverse:
                blk = nblk - 1 - blk
            return (layer, pt_ref[b * n_pages + blk * 2 + which], 0, 0, 0)
        return pl.BlockSpec((None, None, N_HEADS, HEAD_DIM, page), index)

    keys_minor = lambda c: jnp.transpose(c, (0, 1, 3, 4, 2))
    cache_k, cache_v = keys_minor(cache_k), keys_minor(cache_v)

    tok_spec = pl.BlockSpec((None, N_HEADS, QP, HEAD_DIM), lambda b, t, *_: (b, 0, 0, 0))
    fixed = lambda shape: pl.BlockSpec(shape, lambda b, t, *_: (0,) * len(shape))
    blk_bias = pl.BlockSpec((BLOCKS_PER_STEP, N_HEADS, QP, MOBA_BLOCK),
                            lambda b, t, *_: (t, 0, 0, 0))
    step_pages = [page_spec(u, w) for u in range(BLOCKS_PER_STEP) for w in range(2)]
    pages = [cache_k] * len(step_pages) + [cache_v] * len(step_pages)
    page_specs = step_pages + step_pages
    toks = [head_major(q), head_major(k), head_major(v), head_major(gate)]
    tok_specs = [tok_spec] * 4
    vm = lambda *shape: pltpu.VMEM(shape, F32)
    if kind == 0:
        bt, bo = _decode_bias_tables(bias_table, past, ds, nblk)
        body = functools.partial(_moba_decode_kernel, nblk=nblk)
        args = [pt, *toks, *pages, bt, bo]
        in_specs = [*tok_specs, *page_specs, blk_bias, fixed((N_HEADS, QP, LANES))]
        scratch = [vm(nblk, N_HEADS, QP, 1), vm(nblk, N_HEADS, QP, 1), vm(nblk, N_HEADS, QP, 1),
                   vm(nblk, N_HEADS, QP, HEAD_DIM)]
        num_prefetch = 1
    elif kind == 1:
        def umat(n):
            r = jnp.arange(2 * n)[:, None] % n
            return (r > jnp.arange(n)[None, :]).astype(BF16)
        body = functools.partial(_stick_decode_kernel, nblk=nblk)
        args = [pt, *toks, *pages, umat(LANES), umat(MOBA_BLOCK)]
        in_specs = [*tok_specs, *page_specs, fixed((2 * LANES, LANES)),
                    fixed((2 * MOBA_BLOCK, MOBA_BLOCK))]
        scratch = [vm(N_HEADS, QP, 1), vm(N_HEADS, QP, HEAD_DIM)]
        num_prefetch = 1
    else:
        lam, lambda_init, subln_g = extra
        bt, bo = _decode_bias_tables(bias_table, past, ds, nblk)
        body = functools.partial(_diff_decode_kernel, nblk=nblk, lambda_init=lambda_init)
        args = [pt, lam.reshape(1), *toks, *pages, bt, bo, subln_g.reshape(1, 1, HEAD_DIM)]
        in_specs = [*tok_specs, *page_specs, blk_bias, fixed((N_HEADS, QP, LANES)),
                    fixed((1, 1, HEAD_DIM))]
        scratch = [vm(N_HEADS, 2 * QP, 1), vm(N_HEADS, 2 * QP, 1), vm(N_HEADS, 2 * QP, HEAD_DIM)]
        num_prefetch = 2
    o4 = pl.pallas_call(
        body,
        grid_spec=pltpu.PrefetchScalarGridSpec(
            num_scalar_prefetch=num_prefetch,
            grid=(DB, nblk // BLOCKS_PER_STEP),
            in_specs=in_specs,
            out_specs=tok_spec,
            scratch_shapes=scratch),
        out_shape=jax.ShapeDtypeStruct((DB, N_HEADS, QP, HEAD_DIM), F32),
        compiler_params=pltpu.CompilerParams(
            dimension_semantics=("arbitrary", "arbitrary"), vmem_limit_bytes=VMEM_LIMIT),
        name=("moba", "stick", "diff")[kind] + "_decode",
    )(*args)
    return jnp.transpose(o4[:, :, :ds], (0, 2, 1, 3)).reshape(DB * ds, D_MODEL)


def kernel(x_prompt, x_sample, cache_k, cache_v, page_table, bias_table, norm_g, w_in, w_out,
           diff_lambda_q1, diff_lambda_k1, diff_lambda_q2, diff_lambda_k2, diff_subln_g,
           final_norm_g):
    lams = (diff_lambda_q1, diff_lambda_k1, diff_lambda_q2, diff_lambda_k2)
    y_p, k_p, v_p = _prompt_trunk(x_prompt, bias_table, norm_g, w_in, w_out, *lams,
                                  diff_subln_g, final_norm_g)
    y_s, k_s, v_s = _sample_trunk(x_sample, cache_k, cache_v, page_table, bias_table, norm_g,
                                  w_in, w_out, *lams, diff_subln_g, final_norm_g)
    return (y_p, y_s, k_p, v_p, k_s, v_s)
```
